```python
import jax, jax.numpy as jnp
from jax import lax
import numpy as np

D_MODEL = 1024
BATCH = 16
SEQ = 2048
DEPTH = 1

D_PLE = 256
RWKV_WIDTH = D_MODEL // 2
HEAD_SIZE = 64
N_HEADS = RWKV_WIDTH // HEAD_SIZE
POOL_WIDTH = D_MODEL - RWKV_WIDTH
POOL_WINDOWS = (2, 4, 8, 16)
N_POOL_GROUPS = len(POOL_WINDOWS)
POOL_GROUP = POOL_WIDTH // N_POOL_GROUPS
DECAY_LORA = 64
AAA_LORA = 64
GATE_LORA = 128
RWKV_SPLITS = (RWKV_WIDTH, 2 * RWKV_WIDTH, 3 * RWKV_WIDTH, 3 * RWKV_WIDTH + DECAY_LORA, 3 * RWKV_WIDTH + DECAY_LORA + AAA_LORA)
RWKV_IN = 3 * RWKV_WIDTH + DECAY_LORA + AAA_LORA + GATE_LORA
IN_WIDTH = RWKV_IN + POOL_WIDTH
D_FF = -(-8 * D_MODEL // (3 * 256)) * 256
ALPHA = (2.0 * DEPTH) ** 0.25
BETA = (8.0 * DEPTH) ** -0.25
LN_EPS = 1e-5
GN_EPS = 1e-5 * HEAD_SIZE

kernel_name = 'hybrid_rwkv7_pool_deepnorm'


def layer_norm(x, g, b, eps=LN_EPS):
    xf = x.astype(jnp.float32)
    mu = jnp.mean(xf, axis=-1, keepdims=True)
    var = jnp.mean(jnp.square(xf - mu), axis=-1, keepdims=True)
    return ((xf - mu) * lax.rsqrt(var + eps) * g + b).astype(x.dtype)


def token_shift(z):
    return jnp.pad(z, ((0, 0), (1, 0), (0, 0)))[:, :-1]


def wkv7(r, w, k, v, a, b):
    B_, T_, H_, N_ = r.shape

    def step(S, inp):
        r_t, w_t, k_t, v_t, a_t, b_t = inp
        Sa = jnp.einsum('bhvk,bhk->bhv', S, a_t)
        S = S * w_t[:, :, None, :] + Sa[..., None] * b_t[:, :, None, :] + v_t[..., None] * k_t[:, :, None, :]
        y = jnp.einsum('bhvk,bhk->bhv', S, r_t)
        return S, y

    xs = tuple(jnp.moveaxis(t, 1, 0) for t in (r, w, k, v, a, b))
    S0 = jnp.zeros((B_, H_, N_, N_), jnp.float32)
    _, ys = lax.scan(step, S0, xs)
    return jnp.moveaxis(ys, 0, 1)


def rwkv7_time_mix(z, mu, w0, wl_up, a0, al_up, gl_up, k_k, k_a, r_k, lnx_g, lnx_b):
    B_, T_, _ = z.shape
    f32 = jnp.float32
    zs = z + (token_shift(z) - z) * mu
    r, k, v, wd, ad, gd = jnp.split(zs, RWKV_SPLITS, axis=-1)
    w_log = -jax.nn.softplus(-(w0 + jnp.tanh(wd) @ wl_up)) - 0.5
    decay = jnp.exp(-jnp.exp(w_log.astype(f32)))
    a = jax.nn.sigmoid(a0 + ad @ al_up)
    g = jax.nn.sigmoid(gd) @ gl_up

    def heads(t):
        return t.astype(f32).reshape(B_, T_, N_HEADS, HEAD_SIZE)

    kk = heads(k * k_k)
    kk = kk / jnp.maximum(jnp.sqrt(jnp.sum(kk * kk, axis=-1, keepdims=True)), 1e-12)
    k2 = k * (1.0 + (a - 1.0) * k_a)
    rh, kh, vh, ah = heads(r), heads(k2), heads(v), heads(a)
    y = wkv7(rh, heads(decay), kh, vh, -kk, kk * ah)
    m = jnp.mean(y, axis=-1, keepdims=True)
    var = jnp.mean(jnp.square(y - m), axis=-1, keepdims=True)
    y = (y - m) * lax.rsqrt(var + GN_EPS) * lnx_g.reshape(N_HEADS, HEAD_SIZE) + lnx_b.reshape(N_HEADS, HEAD_SIZE)
    y = y + jnp.sum(rh * kh * r_k, axis=-1, keepdims=True) * vh
    y = y.reshape(B_, T_, RWKV_WIDTH) * g.astype(f32)
    return y.astype(z.dtype)


def multiscale_pool(u, pool_w, pool_scale):
    B_, T_, _ = u.shape
    f32 = jnp.float32
    uf = u.astype(f32).reshape(B_, T_, N_POOL_GROUPS, POOL_GROUP)
    c = jnp.cumsum(uf, axis=1)
    pos = jnp.arange(1, T_ + 1, dtype=f32)
    outs = []
    for j, W in enumerate(POOL_WINDOWS):
        cj = c[:, :, j]
        c_prev = jnp.pad(cj, ((0, 0), (W, 0), (0, 0)))[:, :T_]
        cnt = jnp.minimum(pos, float(W))
        outs.append((cj - c_prev) / cnt[None, :, None] - uf[:, :, j])
    d = jnp.stack(outs, axis=2)
    y = jnp.einsum('btgc,gcd->btgd', d, pool_w.astype(f32)).reshape(B_, T_, POOL_WIDTH)
    return (y * pool_scale.astype(f32)).astype(u.dtype)


def setup_inputs(seed: int = 0) -> dict:
    key = jax.random.key(seed)
    ks = jax.random.split(key, 32)
    f32 = jnp.float32
    L, D = DEPTH, D_MODEL

    def nrm(k, shape, scale):
        return jax.random.normal(k, shape, f32) * scale

    lin = jnp.linspace(0.0, 1.0, RWKV_WIDTH, dtype=f32)
    return {
        'x': nrm(ks[0], (BATCH, SEQ, D), 1.0),
        'p': nrm(ks[1], (L, BATCH, SEQ, D_PLE), 1.0),
        'ln_in_g': 1.0 + nrm(ks[2], (D,), 0.02),
        'ln_in_b': nrm(ks[3], (D,), 0.02),
        'w_in': nrm(ks[4], (L, D, IN_WIDTH), D ** -0.5),
        'mu_shift': jax.random.uniform(ks[5], (L, RWKV_IN), f32),
        'w0': (-6.0 + 5.0 * lin ** 0.85)[None, :] + nrm(ks[6], (L, RWKV_WIDTH), 0.1),
        'wl_up': nrm(ks[7], (L, DECAY_LORA, RWKV_WIDTH), 0.1 * DECAY_LORA ** -0.5),
        'a0': nrm(ks[8], (L, RWKV_WIDTH), 0.1),
        'al_up': nrm(ks[9], (L, AAA_LORA, RWKV_WIDTH), 0.1 * AAA_LORA ** -0.5),
        'gl_up': nrm(ks[10], (L, GATE_LORA, RWKV_WIDTH), GATE_LORA ** -0.5),
        'k_k': 0.85 + nrm(ks[11], (L, RWKV_WIDTH), 0.05),
        'k_a': 1.0 + nrm(ks[12], (L, RWKV_WIDTH), 0.05),
        'r_k': nrm(ks[13], (L, N_HEADS, HEAD_SIZE), 0.1),
        'lnx_g': 1.0 + nrm(ks[14], (L, RWKV_WIDTH), 0.02),
        'lnx_b': nrm(ks[15], (L, RWKV_WIDTH), 0.02),
        'pool_w': nrm(ks[16], (L, N_POOL_GROUPS, POOL_GROUP, POOL_GROUP), POOL_GROUP ** -0.5),
        'pool_scale': 1.0 + nrm(ks[17], (L, POOL_WIDTH), 0.1),
        'w_out': nrm(ks[18], (L, D, D), BETA * D ** -0.5),
        'ln1_g': 1.0 + nrm(ks[19], (L, D), 0.02),
        'ln1_b': nrm(ks[20], (L, D), 0.02),
        'ffn_gate': nrm(ks[21], (L, D, D_FF), D ** -0.5),
        'ffn_up': nrm(ks[22], (L, D, D_FF), D ** -0.5),
        'ffn_down': nrm(ks[23], (L, D_FF, D), BETA * D_FF ** -0.5),
        'pl_proj': nrm(ks[24], (L, D_PLE, D), BETA * D_PLE ** -0.5),
        'pl_gate': nrm(ks[25], (L, D, D), D ** -0.5),
        'ln2_g': 1.0 + nrm(ks[26], (L, D), 0.02),
        'ln2_b': nrm(ks[27], (L, D), 0.02),
    }


def reference(x, p, ln_in_g, ln_in_b, w_in, mu_shift, w0, wl_up, a0, al_up, gl_up, k_k, k_a, r_k,
              lnx_g, lnx_b, pool_w, pool_scale, w_out, ln1_g, ln1_b, ffn_gate, ffn_up, ffn_down,
              pl_proj, pl_gate, ln2_g, ln2_b):
    h = layer_norm(x, ln_in_g, ln_in_b)
    for i in range(DEPTH):
        z = h @ w_in[i]
        z_rwkv, u = z[..., :RWKV_IN], z[..., RWKV_IN:]
        y_a = rwkv7_time_mix(z_rwkv, mu_shift[i], w0[i], wl_up[i], a0[i], al_up[i], gl_up[i],
                             k_k[i], k_a[i], r_k[i], lnx_g[i], lnx_b[i])
        y_b = multiscale_pool(u, pool_w[i], pool_scale[i])
        mix = jnp.concatenate([y_a, y_b], axis=-1) @ w_out[i]
        h = layer_norm(ALPHA * h + mix, ln1_g[i], ln1_b[i])
        f = (jax.nn.silu(h @ ffn_gate[i]) * (h @ ffn_up[i])) @ ffn_down[i]
        e = jax.nn.sigmoid(h @ pl_gate[i]) * (p[i] @ pl_proj[i])
        h = layer_norm(ALPHA * h + f + e, ln2_g[i], ln2_b[i])
    return h
```

```python
import functools

import jax
import jax.numpy as jnp
from jax import lax
from jax.experimental import pallas as pl
from jax.experimental.pallas import tpu as pltpu

F32 = jnp.float32
BF16 = jnp.bfloat16

HEAD_SIZE = 64
POOL_WINDOWS = (2, 4, 8, 16)
DECAY_LORA = 64
AAA_LORA = 64
GATE_LORA = 128
LN_EPS = 1e-5
GN_EPS = 1e-5 * HEAD_SIZE

LANES = 128
CHUNK = 64
PAIR = 2 * CHUNK
POOL_HALO = 16
VMEM_LIMIT = 56 * 1024 * 1024


def _dot(a, b):
    return jnp.dot(a, b, preferred_element_type=F32)


def _dot_nt(a, b):
    return lax.dot_general(a, b, (((1,), (1,)), ((), ())), preferred_element_type=F32)


def _layer_norm(x, g, b):
    mu = jnp.mean(x, axis=-1, keepdims=True)
    xc = x - mu
    var = jnp.mean(xc * xc, axis=-1, keepdims=True)
    return xc * lax.rsqrt(var + LN_EPS) * g + b


def _split_dot(x, m):
    hi = x.astype(BF16)
    lo = (x - hi.astype(F32)).astype(BF16)
    return _dot(hi, m) + _dot(lo, m)


def _proj_in_kernel(x_ref, g_ref, b_ref, w_ref, h_ref, zr_ref, u_ref, *, rwkv_in):
    h = _layer_norm(x_ref[...], g_ref[...], b_ref[...])
    h_ref[...] = h
    z = _dot(h.astype(BF16), w_ref[...])
    zr_ref[...] = z[:, :rwkv_in]
    u_ref[...] = z[:, rwkv_in:]


def _wkv_chunk_pair(r, lw, k, v, kk, asig, h_state, c):
    tril, m0, m1, strict, incl, eye = c
    lw_hi = lw.astype(BF16)
    lw_lo = (lw - lw_hi.astype(F32)).astype(BF16)
    cum2 = _dot(tril, jnp.concatenate([lw_hi, lw_lo], axis=1))
    cum = cum2[:, :LANES] + cum2[:, LANES:]
    g = jnp.exp(cum)
    gprev = jnp.exp(cum - lw)
    ginv = jnp.exp(-cum)
    g_last = g[CHUNK - 1:CHUNK, :]
    at = -(kk * gprev)
    rt = r * g
    bt = kk * asig * ginv
    kt = k * ginv

    def stack(x):
        return jnp.concatenate([x * m0, x * m1], axis=0)

    at2, rt2, bt2, kt2, v2 = stack(at), stack(rt), stack(bt), stack(kt), stack(v)
    bl2, kl2 = stack(bt * g_last), stack(kt * g_last)

    a_all = _dot_nt(jnp.concatenate([at2, rt2], axis=0).astype(BF16),
                    jnp.concatenate([bt2, kt2], axis=0).astype(BF16))
    a_ab = jnp.where(strict, a_all[:PAIR, :PAIR], 0.0)
    a_ak = jnp.where(strict, a_all[:PAIR, PAIR:], 0.0)
    a_rb = jnp.where(incl, a_all[PAIR:, :PAIR], 0.0)
    a_rk = jnp.where(incl, a_all[PAIR:, PAIR:], 0.0)

    n_b = a_ab.astype(BF16)
    t = jnp.where(eye, 1.0, 0.0) + a_ab
    s = _dot(n_b, n_b)
    steps = CHUNK.bit_length() - 3
    for _ in range(steps):
        ps = _dot(jnp.concatenate([t, s], axis=0).astype(BF16), s.astype(BF16))
        t = t + ps[:PAIR]
        s = ps[PAIR:]
    t = t + _dot(t.astype(BF16), s.astype(BF16))

    akv = _dot(a_ak.astype(BF16), v2.astype(BF16))
    wu = _dot(t.astype(BF16), jnp.concatenate([at2, akv], axis=1).astype(BF16))
    rhs = jnp.concatenate(
        [wu, jnp.concatenate([jnp.zeros_like(v2), v2], axis=1)], axis=0).astype(BF16)
    top = _dot(jnp.concatenate([bl2, kl2], axis=0).T.astype(BF16), rhs)
    bot = _dot(jnp.concatenate([a_rb, a_rk], axis=1).astype(BF16), rhs)

    p_mat = jnp.where(eye, g_last, 0.0) + top[:, :LANES]
    q_mat = top[:, LANES:]
    g_mat = rt + bot[:CHUNK, :LANES] + bot[CHUNK:, :LANES]
    y0 = bot[:CHUNK, LANES:] + bot[CHUNK:, LANES:]
    h_b = h_state.astype(BF16)
    y = _dot(g_mat.astype(BF16), h_b) + y0
    h_new = _dot(p_mat.astype(BF16), h_b) + q_mat
    return y, h_new


def _rwkv_kernel(z_ref, zprev_ref, mu_ref, w0_ref, a0_ref, lora_ref, gl_ref, kk_ref, ka_ref,
                 rk_ref, lng_ref, lnb_ref, seg_ref, tril_ref, o_ref,
                 st_ref, r_s, lw_s, k_s, v_s, kk_s, as_s, y_s, g_s, bonus_s, *, width, tb_rows):
    tb = pl.program_id(1)

    @pl.when(tb == 0)
    def _():
        st_ref[...] = jnp.zeros_like(st_ref)

    z = z_ref[...]
    prev_row = jnp.where(tb == 0, 0.0, zprev_ref[7:8, :])
    row = lax.broadcasted_iota(jnp.int32, z.shape, 0)
    z_shift = jnp.where(row == 0, prev_row, pltpu.roll(z, 1, 0))
    zs = z + (z_shift - z) * mu_ref[...]
    r = zs[:, 0:width]
    k = zs[:, width:2 * width]
    v = zs[:, 2 * width:3 * width]
    lora_in = zs[:, 3 * width:3 * width + DECAY_LORA + AAA_LORA]
    gd = zs[:, 3 * width + DECAY_LORA + AAA_LORA:]
    lane = lax.broadcasted_iota(jnp.int32, lora_in.shape, 1)
    lora_in = jnp.where(lane < DECAY_LORA, jnp.tanh(lora_in), lora_in)
    lora = _dot(lora_in.astype(BF16), lora_ref[...])
    wx = -(w0_ref[...] + lora[:, :width])
    softplus = jnp.maximum(wx, 0.0) + jnp.log(1.0 + jnp.exp(-jnp.abs(wx)))
    lw = -jnp.exp(-softplus - 0.5)
    asig = jax.nn.sigmoid(a0_ref[...] + lora[:, width:])
    gate = _dot(jax.nn.sigmoid(gd).astype(BF16), gl_ref[...])
    seg = seg_ref[...]
    kk = k * kk_ref[...]
    kk = kk / jnp.maximum(jnp.sqrt(_split_dot(kk * kk, seg)), 1e-12)
    k2 = k * (1.0 + (asig - 1.0) * ka_ref[...])
    bonus = _split_dot(r * k2 * rk_ref[...], seg) * v
    r_s[...] = r
    lw_s[...] = lw
    k_s[...] = k2
    v_s[...] = v
    kk_s[...] = kk
    as_s[...] = asig
    g_s[...] = gate
    bonus_s[...] = bonus

    ri = lax.broadcasted_iota(jnp.int32, (PAIR, PAIR), 0)
    ci = lax.broadcasted_iota(jnp.int32, (PAIR, PAIR), 1)
    lane_row = lax.broadcasted_iota(jnp.int32, (1, LANES), 1)
    m0 = (lane_row < HEAD_SIZE).astype(F32)
    consts = (tril_ref[...], m0, 1.0 - m0, ri > ci, ri >= ci, ri == ci)
    n_pairs = width // LANES

    def chunk_body(ci_, carry):
        rows = pl.ds(pl.multiple_of(ci_ * CHUNK, CHUNK), CHUNK)
        for p in range(n_pairs):
            lanes = slice(p * LANES, (p + 1) * LANES)
            y, h_new = _wkv_chunk_pair(r_s[rows, lanes], lw_s[rows, lanes], k_s[rows, lanes],
                                       v_s[rows, lanes], kk_s[rows, lanes], as_s[rows, lanes],
                                       st_ref[p], consts)
            y_s[rows, lanes] = y
            st_ref[p] = h_new
        return carry

    lax.fori_loop(0, tb_rows // CHUNK, chunk_body, 0)

    y = y_s[...]
    inv_n = 1.0 / HEAD_SIZE
    mean = _split_dot(y, seg) * inv_n
    yc = y - mean
    var = _split_dot(yc * yc, seg) * inv_n
    yn = yc * lax.rsqrt(var + GN_EPS) * lng_ref[...] + lnb_ref[...]
    o_ref[...] = (yn + bonus_s[...]) * g_s[...]


def _pool_kernel(u_ref, uprev_ref, pw_ref, ps_ref, o_ref, *, tb_rows):
    tb = pl.program_id(1)
    cur = u_ref[...]
    prev = jnp.where(tb == 0, 0.0, uprev_ref[...])
    ext = jnp.concatenate([prev, cur], axis=0)
    pos = (tb * tb_rows + 1 + lax.broadcasted_iota(jnp.int32, (tb_rows, 1), 0)).astype(F32)
    outs = []
    for j, win in enumerate(POOL_WINDOWS):
        lanes = slice(j * LANES, (j + 1) * LANES)
        s = ext[:, lanes]
        shift = 1
        while shift < win:
            s = s + pltpu.roll(s, shift, 0)
            shift *= 2
        d = s[POOL_HALO:, :] / jnp.minimum(pos, float(win)) - cur[:, lanes]
        outs.append(_dot(d.astype(BF16), pw_ref[j]))
    o_ref[...] = jnp.concatenate(outs, axis=1) * ps_ref[...]


def _mix_kernel(h_ref, ya_ref, yb_ref, wo_ref, g_ref, b_ref, o_ref, *, alpha, width):
    mix = (_dot(ya_ref[...].astype(BF16), wo_ref[:width, :])
           + _dot(yb_ref[...].astype(BF16), wo_ref[width:, :]))
    o_ref[...] = _layer_norm(alpha * h_ref[...] + mix, g_ref[...], b_ref[...])


def _ffn_kernel(h_ref, p_ref, wg_ref, wu_ref, wd_ref, wpg_ref, wpp_ref, g_ref, b_ref, o_ref,
                *, alpha, ff_tile):
    h1 = h_ref[...]
    hb = h1.astype(BF16)
    acc = alpha * h1 + jax.nn.sigmoid(_dot(hb, wpg_ref[...])) * _dot(p_ref[...].astype(BF16),
                                                                      wpp_ref[...])
    d_ff = wg_ref.shape[1]
    for c in range(d_ff // ff_tile):
        cols = slice(c * ff_tile, (c + 1) * ff_tile)
        gate = _dot(hb, wg_ref[:, cols])
        up = _dot(hb, wu_ref[:, cols])
        act = gate * jax.nn.sigmoid(gate) * up
        acc = acc + _dot(act.astype(BF16), wd_ref[cols, :])
    o_ref[...] = _layer_norm(acc, g_ref[...], b_ref[...])


def _full(shape):
    return pl.BlockSpec(shape, lambda *_: (0,) * len(shape))


def _params(semantics):
    return pltpu.CompilerParams(dimension_semantics=semantics, vmem_limit_bytes=VMEM_LIMIT)


def _layer(h_in, p_l, x_is_raw, ln_in, prm, *, batch, seq):
    n_tok, d_model = h_in.shape
    width = prm['w0'].shape[-1]
    rwkv_in = 3 * width + DECAY_LORA + AAA_LORA + GATE_LORA
    depth = prm['depth']
    alpha = (2.0 * depth) ** 0.25

    tm = 512
    row = lambda c: pl.BlockSpec((tm, c), lambda i: (i, 0))
    w_in = prm['w_in'].astype(BF16)
    h, z_r, u = pl.pallas_call(
        functools.partial(_proj_in_kernel, rwkv_in=rwkv_in),
        grid=(n_tok // tm,),
        in_specs=[row(d_model), _full((1, d_model)), _full((1, d_model)), _full(w_in.shape)],
        out_specs=[row(d_model), row(rwkv_in), row(width)],
        out_shape=[jax.ShapeDtypeStruct((n_tok, d_model), F32),
                   jax.ShapeDtypeStruct((n_tok, rwkv_in), F32),
                   jax.ShapeDtypeStruct((n_tok, width), F32)],
        compiler_params=_params(("parallel",)),
        name="proj_in",
    )(h_in, ln_in[0].reshape(1, -1), ln_in[1].reshape(1, -1), w_in)

    tb_rows = 512
    n_tb = seq // tb_rows
    lora_w = jnp.zeros((DECAY_LORA + AAA_LORA, 2 * width), F32)
    lora_w = lora_w.at[:DECAY_LORA, :width].set(prm['wl_up'])
    lora_w = lora_w.at[DECAY_LORA:, width:].set(prm['al_up']).astype(BF16)
    head_id = jnp.arange(width) // HEAD_SIZE
    seg = (head_id[:, None] == head_id[None, :]).astype(BF16)
    tril = jnp.tril(jnp.ones((CHUNK, CHUNK), BF16))
    vec = lambda a: a.reshape(1, -1)
    tok = lambda c: pl.BlockSpec((tb_rows, c), lambda b, t: (b * n_tb + t, 0))
    prev8 = pl.BlockSpec(
        (8, rwkv_in), lambda b, t: (jnp.maximum((b * n_tb + t) * (tb_rows // 8) - 1, 0), 0))
    y_a = pl.pallas_call(
        functools.partial(_rwkv_kernel, width=width, tb_rows=tb_rows),
        grid=(batch, n_tb),
        in_specs=[tok(rwkv_in), prev8, _full((1, rwkv_in)), _full((1, width)), _full((1, width)),
                  _full(lora_w.shape), _full((GATE_LORA, width)), _full((1, width)),
                  _full((1, width)), _full((1, width)), _full((1, width)), _full((1, width)),
                  _full(seg.shape), _full(tril.shape)],
        out_specs=tok(width),
        out_shape=jax.ShapeDtypeStruct((n_tok, width), F32),
        scratch_shapes=[pltpu.VMEM((width // LANES, LANES, LANES), F32)]
        + [pltpu.VMEM((tb_rows, width), F32)] * 9,
        compiler_params=_params(("arbitrary", "arbitrary")),
        name="rwkv",
    )(z_r, z_r, vec(prm['mu_shift']), vec(prm['w0']), vec(prm['a0']), lora_w,
      prm['gl_up'].astype(BF16), vec(prm['k_k']), vec(prm['k_a']), vec(prm['r_k']),
      vec(prm['lnx_g']), vec(prm['lnx_b']), seg, tril)

    halo = pl.BlockSpec(
        (POOL_HALO, width),
        lambda b, t: (jnp.maximum((b * n_tb + t) * (tb_rows // POOL_HALO) - 1, 0), 0))
    y_b = pl.pallas_call(
        functools.partial(_pool_kernel, tb_rows=tb_rows),
        grid=(batch, n_tb),
        in_specs=[tok(width), halo, _full(prm['pool_w'].shape), _full((1, width))],
        out_specs=tok(width),
        out_shape=jax.ShapeDtypeStruct((n_tok, width), F32),
        compiler_params=_params(("parallel", "parallel")),
        name="pool",
    )(u, u, prm['pool_w'].astype(BF16), vec(prm['pool_scale']))

    w_out = prm['w_out'].astype(BF16)
    h1 = pl.pallas_call(
        functools.partial(_mix_kernel, alpha=alpha, width=width),
        grid=(n_tok // tm,),
        in_specs=[row(d_model), row(width), row(width), _full(w_out.shape),
                  _full((1, d_model)), _full((1, d_model))],
        out_specs=row(d_model),
        out_shape=jax.ShapeDtypeStruct((n_tok, d_model), F32),
        compiler_params=_params(("parallel",)),
        name="mix",
    )(h, y_a, y_b, w_out, vec(prm['ln1_g']), vec(prm['ln1_b']))

    d_ff = prm['ffn_gate'].shape[-1]
    d_ple = p_l.shape[-1]
    tf = 256
    weights = [prm['ffn_gate'].astype(BF16), prm['ffn_up'].astype(BF16),
               prm['ffn_down'].astype(BF16), prm['pl_gate'].astype(BF16),
               prm['pl_proj'].astype(BF16)]
    out = pl.pallas_call(
        functools.partial(_ffn_kernel, alpha=alpha, ff_tile=d_ff // 2),
        grid=(n_tok // tf,),
        in_specs=[pl.BlockSpec((tf, d_model), lambda i: (i, 0)),
                  pl.BlockSpec((tf, d_ple), lambda i: (i, 0))]
        + [_full(w.shape) for w in weights] + [_full((1, d_model)), _full((1, d_model))],
        out_specs=pl.BlockSpec((tf, d_model), lambda i: (i, 0)),
        out_shape=jax.ShapeDtypeStruct((n_tok, d_model), F32),
        compiler_params=_params(("parallel",)),
        name="ffn",
    )(h1, p_l, *weights, vec(prm['ln2_g']), vec(prm['ln2_b']))
    return out


def kernel(x, p, ln_in_g, ln_in_b, w_in, mu_shift, w0, wl_up, a0, al_up, gl_up, k_k, k_a, r_k,
           lnx_g, lnx_b, pool_w, pool_scale, w_out, ln1_g, ln1_b, ffn_gate, ffn_up, ffn_down,
           pl_proj, pl_gate, ln2_g, ln2_b):
    batch, seq, d_model = x.shape
    depth = p.shape[0]
    assert depth == 1, "the fused input norm assumes a single layer"
    stacked = dict(w_in=w_in, mu_shift=mu_shift, w0=w0, wl_up=wl_up, a0=a0, al_up=al_up,
                   gl_up=gl_up, k_k=k_k, k_a=k_a, r_k=r_k, lnx_g=lnx_g, lnx_b=lnx_b,
                   pool_w=pool_w, pool_scale=pool_scale, w_out=w_out, ln1_g=ln1_g, ln1_b=ln1_b,
                   ffn_gate=ffn_gate, ffn_up=ffn_up, ffn_down=ffn_down, pl_proj=pl_proj,
                   pl_gate=pl_gate, ln2_g=ln2_g, ln2_b=ln2_b)
    prm = {name: t[0] for name, t in stacked.items()}
    prm['depth'] = depth
    out = _layer(x.reshape(batch * seq, d_model), p[0].reshape(batch * seq, -1), True,
                 (ln_in_g, ln_in_b), prm, batch=batch, seq=seq)
    return out.reshape(batch, seq, d_model)
```

```python
import functools

import jax
import jax.numpy as jnp
from jax import lax
from jax.experimental import pallas as pl
from jax.experimental.pallas import tpu as pltpu

F32 = jnp.float32
BF16 = jnp.bfloat16

HEAD_SIZE = 64
POOL_WINDOWS = (2, 4, 8, 16)
DECAY_LORA = 64
AAA_LORA = 64
GATE_LORA = 128
LN_EPS = 1e-5
GN_EPS = 1e-5 * HEAD_SIZE

LANES = 128
CHUNK = 64
PAIR = 2 * CHUNK
CHUNKS_PER_ITER = 4
POOL_HALO = 16
VMEM_LIMIT = 56 * 1024 * 1024


def _dot(a, b):
    return jnp.dot(a, b, preferred_element_type=F32)


def _dot_nt(a, b):
    return lax.dot_general(a, b, (((1,), (1,)), ((), ())), preferred_element_type=F32)


def _layer_norm(x, g, b):
    mu = jnp.mean(x, axis=-1, keepdims=True)
    xc = x - mu
    var = jnp.mean(xc * xc, axis=-1, keepdims=True)
    return xc * lax.rsqrt(var + LN_EPS) * g + b


def _split_dot(x, m):
    hi = x.astype(BF16)
    lo = (x - hi.astype(F32)).astype(BF16)
    return _dot(hi, m) + _dot(lo, m)


def _proj_in_kernel(x_ref, g_ref, b_ref, w_ref, h_ref, zr_ref, u_ref, *, rwkv_in):
    h = _layer_norm(x_ref[...], g_ref[...], b_ref[...])
    h_ref[...] = h
    z = _dot(h.astype(BF16), w_ref[...])
    zr_ref[...] = z[:, :rwkv_in]
    u_ref[...] = z[:, rwkv_in:]


def _each(fn, *lists):
    return [fn(*xs) for xs in zip(*lists)]


def _wkv_chunk(r, lw, k, v, kk, asig, c):
    tril, m0, m1, strict, incl, eye = c

    def cumsum(x):
        hi = x.astype(BF16)
        lo = (x - hi.astype(F32)).astype(BF16)
        both = _dot(tril, jnp.concatenate([hi, lo], axis=1))
        return both[:, :LANES] + both[:, LANES:]

    def stack(x):
        return jnp.concatenate([x * m0, x * m1], axis=0)

    cum = _each(cumsum, lw)
    g = _each(jnp.exp, cum)
    gprev = _each(lambda cu, x: jnp.exp(cu - x), cum, lw)
    ginv = _each(lambda cu: jnp.exp(-cu), cum)
    g_last = _each(lambda x: x[CHUNK - 1:CHUNK, :], g)
    rt = _each(lambda x, y: x * y, r, g)
    bt = _each(lambda x, y, z: x * y * z, kk, asig, ginv)
    kt = _each(lambda x, y: x * y, k, ginv)
    at2 = _each(lambda x, y: stack(-(x * y)), kk, gprev)
    rt2, bt2, kt2, v2 = _each(stack, rt), _each(stack, bt), _each(stack, kt), _each(stack, v)
    bl2 = _each(lambda x, y: stack(x * y), bt, g_last)
    kl2 = _each(lambda x, y: stack(x * y), kt, g_last)

    a_all = _each(lambda a_, r_, b_, k_: _dot_nt(
        jnp.concatenate([a_, r_], axis=0).astype(BF16),
        jnp.concatenate([b_, k_], axis=0).astype(BF16)), at2, rt2, bt2, kt2)
    a_ab = _each(lambda a_: jnp.where(strict, a_[:PAIR, :PAIR], 0.0), a_all)
    a_ak = _each(lambda a_: jnp.where(strict, a_[:PAIR, PAIR:], 0.0), a_all)
    a_rb = _each(lambda a_: jnp.where(incl, a_[PAIR:, :PAIR], 0.0), a_all)
    a_rk = _each(lambda a_: jnp.where(incl, a_[PAIR:, PAIR:], 0.0), a_all)

    t = _each(lambda n: jnp.where(eye, 1.0, 0.0) + n, a_ab)
    s = _each(lambda n: _dot(n.astype(BF16), n.astype(BF16)), a_ab)
    akv = _each(lambda a_, v_: _dot(a_.astype(BF16), v_.astype(BF16)), a_ak, v2)
    for _ in range(CHUNK.bit_length() - 3):
        ps = _each(lambda t_, s_: _dot(jnp.concatenate([t_, s_], axis=0).astype(BF16),
                                       s_.astype(BF16)), t, s)
        t = _each(lambda t_, ps_: t_ + ps_[:PAIR], t, ps)
        s = _each(lambda ps_: ps_[PAIR:], ps)
    t = _each(lambda t_, s_: t_ + _dot(t_.astype(BF16), s_.astype(BF16)), t, s)

    wu = _each(lambda t_, a_, akv_: _dot(
        t_.astype(BF16), jnp.concatenate([a_, akv_], axis=1).astype(BF16)), t, at2, akv)
    rhs = _each(lambda wu_, v_: jnp.concatenate(
        [wu_, jnp.concatenate([jnp.zeros_like(v_), v_], axis=1)], axis=0).astype(BF16), wu, v2)
    top = _each(lambda b_, k_, rhs_: _dot(
        jnp.concatenate([b_, k_], axis=0).T.astype(BF16), rhs_), bl2, kl2, rhs)
    bot = _each(lambda b_, k_, rhs_: _dot(
        jnp.concatenate([b_, k_], axis=1).astype(BF16), rhs_), a_rb, a_rk, rhs)

    p_mat = _each(lambda gl, top_: (jnp.where(eye, gl, 0.0) + top_[:, :LANES]).astype(BF16),
                  g_last, top)
    q_mat = _each(lambda top_: top_[:, LANES:], top)
    g_mat = _each(lambda rt_, b_: (rt_ + b_[:CHUNK, :LANES] + b_[CHUNK:, :LANES]).astype(BF16),
                  rt, bot)
    y0 = _each(lambda b_: b_[:CHUNK, LANES:] + b_[CHUNK:, LANES:], bot)
    return p_mat, q_mat, g_mat, y0


def _rwkv_kernel(z_ref, zprev_ref, mu_ref, w0_ref, a0_ref, lora_ref, gl_ref, kk_ref, ka_ref,
                 rk_ref, lng_ref, lnb_ref, seg_ref, tril_ref, o_ref,
                 st_ref, r_s, lw_s, k_s, v_s, kk_s, as_s, y_s, g_s, bonus_s, *, width, tb_rows):
    tb = pl.program_id(1)

    @pl.when(tb == 0)
    def _():
        st_ref[...] = jnp.zeros_like(st_ref)

    z = z_ref[...]
    prev_row = jnp.where(tb == 0, 0.0, zprev_ref[7:8, :])
    row = lax.broadcasted_iota(jnp.int32, z.shape, 0)
    z_shift = jnp.where(row == 0, prev_row, pltpu.roll(z, 1, 0))
    zs = z + (z_shift - z) * mu_ref[...]
    r = zs[:, 0:width]
    k = zs[:, width:2 * width]
    v = zs[:, 2 * width:3 * width]
    lora_in = zs[:, 3 * width:3 * width + DECAY_LORA + AAA_LORA]
    gd = zs[:, 3 * width + DECAY_LORA + AAA_LORA:]
    lane = lax.broadcasted_iota(jnp.int32, lora_in.shape, 1)
    lora_in = jnp.where(lane < DECAY_LORA, jnp.tanh(lora_in), lora_in)
    lora = _dot(lora_in.astype(BF16), lora_ref[...])
    wx = -(w0_ref[...] + lora[:, :width])
    softplus = jnp.maximum(wx, 0.0) + jnp.log(1.0 + jnp.exp(-jnp.abs(wx)))
    lw = -jnp.exp(-softplus - 0.5)
    asig = jax.nn.sigmoid(a0_ref[...] + lora[:, width:])
    gate = _dot(jax.nn.sigmoid(gd).astype(BF16), gl_ref[...])
    seg = seg_ref[...]
    kk = k * kk_ref[...]
    kk = kk / jnp.maximum(jnp.sqrt(_split_dot(kk * kk, seg)), 1e-12)
    k2 = k * (1.0 + (asig - 1.0) * ka_ref[...])
    bonus = _split_dot(r * k2 * rk_ref[...], seg) * v
    r_s[...] = r
    lw_s[...] = lw
    k_s[...] = k2
    v_s[...] = v
    kk_s[...] = kk
    as_s[...] = asig
    g_s[...] = gate
    bonus_s[...] = bonus

    ri = lax.broadcasted_iota(jnp.int32, (PAIR, PAIR), 0)
    ci = lax.broadcasted_iota(jnp.int32, (PAIR, PAIR), 1)
    lane_row = lax.broadcasted_iota(jnp.int32, (1, LANES), 1)
    m0 = (lane_row < HEAD_SIZE).astype(F32)
    consts = (tril_ref[...], m0, 1.0 - m0, ri > ci, ri >= ci, ri == ci)
    n_pairs = width // LANES

    def chunk_body(ci_, carry):
        rows = [pl.ds(pl.multiple_of((ci_ * CHUNKS_PER_ITER + u) * CHUNK, CHUNK), CHUNK)
                for u in range(CHUNKS_PER_ITER)]
        lanes = [slice(p * LANES, (p + 1) * LANES) for p in range(n_pairs)]
        load = lambda ref: [ref[rw, ln] for rw in rows for ln in lanes]
        p_mat, q_mat, g_mat, y0 = _wkv_chunk(load(r_s), load(lw_s), load(k_s), load(v_s),
                                             load(kk_s), load(as_s), consts)
        h = [st_ref[p] for p in range(n_pairs)]
        for u in range(CHUNKS_PER_ITER):
            for p in range(n_pairs):
                i = u * n_pairs + p
                h_b = h[p].astype(BF16)
                y_s[rows[u], lanes[p]] = _dot(g_mat[i], h_b) + y0[i]
                h[p] = _dot(p_mat[i], h_b) + q_mat[i]
        for p in range(n_pairs):
            st_ref[p] = h[p]
        return carry

    lax.fori_loop(0, tb_rows // (CHUNK * CHUNKS_PER_ITER), chunk_body, 0)

    y = y_s[...]
    inv_n = 1.0 / HEAD_SIZE
    mean = _split_dot(y, seg) * inv_n
    yc = y - mean
    var = _split_dot(yc * yc, seg) * inv_n
    yn = yc * lax.rsqrt(var + GN_EPS) * lng_ref[...] + lnb_ref[...]
    o_ref[...] = (yn + bonus_s[...]) * g_s[...]


def _pool_kernel(u_ref, uprev_ref, pw_ref, ps_ref, o_ref, *, tb_rows):
    tb = pl.program_id(1)
    cur = u_ref[...]
    prev = jnp.where(tb == 0, 0.0, uprev_ref[...])
    ext = jnp.concatenate([prev, cur], axis=0)
    pos = (tb * tb_rows + 1 + lax.broadcasted_iota(jnp.int32, (tb_rows, 1), 0)).astype(F32)
    outs = []
    for j, win in enumerate(POOL_WINDOWS):
        lanes = slice(j * LANES, (j + 1) * LANES)
        s = ext[:, lanes]
        shift = 1
        while shift < win:
            s = s + pltpu.roll(s, shift, 0)
            shift *= 2
        d = s[POOL_HALO:, :] / jnp.minimum(pos, float(win)) - cur[:, lanes]
        outs.append(_dot(d.astype(BF16), pw_ref[j]))
    o_ref[...] = jnp.concatenate(outs, axis=1) * ps_ref[...]


def _mix_kernel(h_ref, ya_ref, yb_ref, wo_ref, g_ref, b_ref, o_ref, *, alpha, width):
    mix = (_dot(ya_ref[...].astype(BF16), wo_ref[:width, :])
           + _dot(yb_ref[...].astype(BF16), wo_ref[width:, :]))
    o_ref[...] = _layer_norm(alpha * h_ref[...] + mix, g_ref[...], b_ref[...])


def _ffn_kernel(h_ref, p_ref, wg_ref, wu_ref, wd_ref, wpg_ref, wpp_ref, g_ref, b_ref, o_ref,
                *, alpha, ff_tile):
    h1 = h_ref[...]
    hb = h1.astype(BF16)
    acc = alpha * h1 + jax.nn.sigmoid(_dot(hb, wpg_ref[...])) * _dot(p_ref[...].astype(BF16),
                                                                      wpp_ref[...])
    d_ff = wg_ref.shape[1]
    for c in range(d_ff // ff_tile):
        cols = slice(c * ff_tile, (c + 1) * ff_tile)
        gate = _dot(hb, wg_ref[:, cols])
        up = _dot(hb, wu_ref[:, cols])
        act = gate * jax.nn.sigmoid(gate) * up
        acc = acc + _dot(act.astype(BF16), wd_ref[cols, :])
    o_ref[...] = _layer_norm(acc, g_ref[...], b_ref[...])


def _full(shape):
    return pl.BlockSpec(shape, lambda *_: (0,) * len(shape))


def _params(semantics):
    return pltpu.CompilerParams(dimension_semantics=semantics, vmem_limit_bytes=VMEM_LIMIT)


def _layer(h_in, p_l, ln_in, prm, *, batch, seq):
    n_tok, d_model = h_in.shape
    width = prm['w0'].shape[-1]
    rwkv_in = 3 * width + DECAY_LORA + AAA_LORA + GATE_LORA
    depth = prm['depth']
    alpha = (2.0 * depth) ** 0.25

    tm = 512
    row = lambda c: pl.BlockSpec((tm, c), lambda i: (i, 0))
    w_in = prm['w_in'].astype(BF16)
    h, z_r, u = pl.pallas_call(
        functools.partial(_proj_in_kernel, rwkv_in=rwkv_in),
        grid=(n_tok // tm,),
        in_specs=[row(d_model), _full((1, d_model)), _full((1, d_model)), _full(w_in.shape)],
        out_specs=[row(d_model), row(rwkv_in), row(width)],
        out_shape=[jax.ShapeDtypeStruct((n_tok, d_model), F32),
                   jax.ShapeDtypeStruct((n_tok, rwkv_in), F32),
                   jax.ShapeDtypeStruct((n_tok, width), F32)],
        compiler_params=_params(("parallel",)),
        name="proj_in",
    )(h_in, ln_in[0].reshape(1, -1), ln_in[1].reshape(1, -1), w_in)

    tb_rows = 512
    n_tb = seq // tb_rows
    lora_w = jnp.zeros((DECAY_LORA + AAA_LORA, 2 * width), F32)
    lora_w = lora_w.at[:DECAY_LORA, :width].set(prm['wl_up'])
    lora_w = lora_w.at[DECAY_LORA:, width:].set(prm['al_up']).astype(BF16)
    head_id = jnp.arange(width) // HEAD_SIZE
    seg = (head_id[:, None] == head_id[None, :]).astype(BF16)
    tril = jnp.tril(jnp.ones((CHUNK, CHUNK), BF16))
    vec = lambda a: a.reshape(1, -1)
    tok = lambda c: pl.BlockSpec((tb_rows, c), lambda b, t: (b * n_tb + t, 0))
    prev8 = pl.BlockSpec(
        (8, rwkv_in), lambda b, t: (jnp.maximum((b * n_tb + t) * (tb_rows // 8) - 1, 0), 0))
    y_a = pl.pallas_call(
        functools.partial(_rwkv_kernel, width=width, tb_rows=tb_rows),
        grid=(batch, n_tb),
        in_specs=[tok(rwkv_in), prev8, _full((1, rwkv_in)), _full((1, width)), _full((1, width)),
                  _full(lora_w.shape), _full((GATE_LORA, width)), _full((1, width)),
                  _full((1, width)), _full((1, width)), _full((1, width)), _full((1, width)),
                  _full(seg.shape), _full(tril.shape)],
        out_specs=tok(width),
        out_shape=jax.ShapeDtypeStruct((n_tok, width), F32),
        scratch_shapes=[pltpu.VMEM((width // LANES, LANES, LANES), F32)]
        + [pltpu.VMEM((tb_rows, width), F32)] * 9,
        compiler_params=_params(("arbitrary", "arbitrary")),
        name="rwkv",
    )(z_r, z_r, vec(prm['mu_shift']), vec(prm['w0']), vec(prm['a0']), lora_w,
      prm['gl_up'].astype(BF16), vec(prm['k_k']), vec(prm['k_a']), vec(prm['r_k']),
      vec(prm['lnx_g']), vec(prm['lnx_b']), seg, tril)

    halo = pl.BlockSpec(
        (POOL_HALO, width),
        lambda b, t: (jnp.maximum((b * n_tb + t) * (tb_rows // POOL_HALO) - 1, 0), 0))
    y_b = pl.pallas_call(
        functools.partial(_pool_kernel, tb_rows=tb_rows),
        grid=(batch, n_tb),
        in_specs=[tok(width), halo, _full(prm['pool_w'].shape), _full((1, width))],
        out_specs=tok(width),
        out_shape=jax.ShapeDtypeStruct((n_tok, width), F32),
        compiler_params=_params(("parallel", "parallel")),
        name="pool",
    )(u, u, prm['pool_w'].astype(BF16), vec(prm['pool_scale']))

    w_out = prm['w_out'].astype(BF16)
    h1 = pl.pallas_call(
        functools.partial(_mix_kernel, alpha=alpha, width=width),
        grid=(n_tok // tm,),
        in_specs=[row(d_model), row(width), row(width), _full(w_out.shape),
                  _full((1, d_model)), _full((1, d_model))],
        out_specs=row(d_model),
        out_shape=jax.ShapeDtypeStruct((n_tok, d_model), F32),
        compiler_params=_params(("parallel",)),
        name="mix",
    )(h, y_a, y_b, w_out, vec(prm['ln1_g']), vec(prm['ln1_b']))

    d_ff = prm['ffn_gate'].shape[-1]
    d_ple = p_l.shape[-1]
    tf = 256
    weights = [prm['ffn_gate'].astype(BF16), prm['ffn_up'].astype(BF16),
               prm['ffn_down'].astype(BF16), prm['pl_gate'].astype(BF16),
               prm['pl_proj'].astype(BF16)]
    out = pl.pallas_call(
        functools.partial(_ffn_kernel, alpha=alpha, ff_tile=d_ff // 2),
        grid=(n_tok // tf,),
        in_specs=[pl.BlockSpec((tf, d_model), lambda i: (i, 0)),
                  pl.BlockSpec((tf, d_ple), lambda i: (i, 0))]
        + [_full(w.shape) for w in weights] + [_full((1, d_model)), _full((1, d_model))],
        out_specs=pl.BlockSpec((tf, d_model), lambda i: (i, 0)),
        out_shape=jax.ShapeDtypeStruct((n_tok, d_model), F32),
        compiler_params=_params(("parallel",)),
        name="ffn",
    )(h1, p_l, *weights, vec(prm['ln2_g']), vec(prm['ln2_b']))
    return out


def kernel(x, p, ln_in_g, ln_in_b, w_in, mu_shift, w0, wl_up, a0, al_up, gl_up, k_k, k_a, r_k,
           lnx_g, lnx_b, pool_w, pool_scale, w_out, ln1_g, ln1_b, ffn_gate, ffn_up, ffn_down,
           pl_proj, pl_gate, ln2_g, ln2_b):
    batch, seq, d_model = x.shape
    depth = p.shape[0]
    assert depth == 1, "the fused input norm assumes a single layer"
    stacked = dict(w_in=w_in, mu_shift=mu_shift, w0=w0, wl_up=wl_up, a0=a0, al_up=al_up,
                   gl_up=gl_up, k_k=k_k, k_a=k_a, r_k=r_k, lnx_g=lnx_g, lnx_b=lnx_b,
                   pool_w=pool_w, pool_scale=pool_scale, w_out=w_out, ln1_g=ln1_g, ln1_b=ln1_b,
                   ffn_gate=ffn_gate, ffn_up=ffn_up, ffn_down=ffn_down, pl_proj=pl_proj,
                   pl_gate=pl_gate, ln2_g=ln2_g, ln2_b=ln2_b)
    prm = {name: t[0] for name, t in stacked.items()}
    prm['depth'] = depth
    out = _layer(x.reshape(batch * seq, d_model), p[0].reshape(batch * seq, -1),
                 (ln_in_g, ln_in_b), prm, batch=batch, seq=seq)
    return out.reshape(batch, seq, d_model)
```

```python
import functools

import jax
import jax.numpy as jnp
from jax import lax
from jax.experimental import pallas as pl
from jax.experimental.pallas import tpu as pltpu

F32 = jnp.float32
BF16 = jnp.bfloat16

HEAD_SIZE = 64
POOL_WINDOWS = (2, 4, 8, 16)
DECAY_LORA = 64
AAA_LORA = 64
GATE_LORA = 128
LN_EPS = 1e-5
GN_EPS = 1e-5 * HEAD_SIZE

LANES = 128
CHUNK = 64
PAIR = 2 * CHUNK
CHUNKS_PER_ITER = 4
POOL_HALO = 16
VMEM_LIMIT = 56 * 1024 * 1024


def _dot(a, b):
    return jnp.dot(a, b, preferred_element_type=F32)


def _dot_nt(a, b):
    return lax.dot_general(a, b, (((1,), (1,)), ((), ())), preferred_element_type=F32)


def _layer_norm(x, g, b):
    mu = jnp.mean(x, axis=-1, keepdims=True)
    xc = x - mu
    var = jnp.mean(xc * xc, axis=-1, keepdims=True)
    return xc * lax.rsqrt(var + LN_EPS) * g + b


def _split_dot(x, m):
    hi = x.astype(BF16)
    lo = (x - hi.astype(F32)).astype(BF16)
    return _dot(hi, m) + _dot(lo, m)


def _proj_in_kernel(x_ref, g_ref, b_ref, w_ref, h_ref, zr_ref, u_ref, *, rwkv_in):
    h = _layer_norm(x_ref[...], g_ref[...], b_ref[...])
    h_ref[...] = h
    z = _dot(h.astype(BF16), w_ref[...])
    zr_ref[...] = z[:, :rwkv_in]
    u_ref[...] = z[:, rwkv_in:]


def _each(fn, *lists):
    return [fn(*xs) for xs in zip(*lists)]


def _wkv_chunk(r, lw, k, v, kk, asig, c):
    tril, m0, m1, strict, incl, eye = c

    def cumsum(x):
        hi = x.astype(BF16)
        lo = (x - hi.astype(F32)).astype(BF16)
        both = _dot(tril, jnp.concatenate([hi, lo], axis=1))
        return both[:, :LANES] + both[:, LANES:]

    def stack(x):
        return jnp.concatenate([x * m0, x * m1], axis=0)

    cum = _each(cumsum, lw)
    g = _each(jnp.exp, cum)
    gprev = _each(lambda cu, x: jnp.exp(cu - x), cum, lw)
    ginv = _each(lambda cu: jnp.exp(-cu), cum)
    g_last = _each(lambda x: x[CHUNK - 1:CHUNK, :], g)
    rt = _each(lambda x, y: x * y, r, g)
    bt = _each(lambda x, y, z: x * y * z, kk, asig, ginv)
    kt = _each(lambda x, y: x * y, k, ginv)
    at2 = _each(lambda x, y: stack(-(x * y)), kk, gprev)
    rt2, bt2, kt2, v2 = _each(stack, rt), _each(stack, bt), _each(stack, kt), _each(stack, v)
    bl2 = _each(lambda x, y: stack(x * y), bt, g_last)
    kl2 = _each(lambda x, y: stack(x * y), kt, g_last)

    a_all = _each(lambda a_, r_, b_, k_: _dot_nt(
        jnp.concatenate([a_, r_], axis=0).astype(BF16),
        jnp.concatenate([b_, k_], axis=0).astype(BF16)), at2, rt2, bt2, kt2)
    a_ab = _each(lambda a_: jnp.where(strict, a_[:PAIR, :PAIR], 0.0), a_all)
    a_ak = _each(lambda a_: jnp.where(strict, a_[:PAIR, PAIR:], 0.0), a_all)
    a_rb = _each(lambda a_: jnp.where(incl, a_[PAIR:, :PAIR], 0.0), a_all)
    a_rk = _each(lambda a_: jnp.where(incl, a_[PAIR:, PAIR:], 0.0), a_all)

    t = _each(lambda n: jnp.where(eye, 1.0, 0.0) + n, a_ab)
    s = _each(lambda n: _dot(n.astype(BF16), n.astype(BF16)), a_ab)
    akv = _each(lambda a_, v_: _dot(a_.astype(BF16), v_.astype(BF16)), a_ak, v2)
    for _ in range(CHUNK.bit_length() - 3):
        ps = _each(lambda t_, s_: _dot(jnp.concatenate([t_, s_], axis=0).astype(BF16),
                                       s_.astype(BF16)), t, s)
        t = _each(lambda t_, ps_: t_ + ps_[:PAIR], t, ps)
        s = _each(lambda ps_: ps_[PAIR:], ps)
    t = _each(lambda t_, s_: t_ + _dot(t_.astype(BF16), s_.astype(BF16)), t, s)

    wu = _each(lambda t_, a_, akv_: _dot(
        t_.astype(BF16), jnp.concatenate([a_, akv_], axis=1).astype(BF16)), t, at2, akv)
    rhs = _each(lambda wu_, v_: jnp.concatenate(
        [wu_, jnp.concatenate([jnp.zeros_like(v_), v_], axis=1)], axis=0).astype(BF16), wu, v2)
    top = _each(lambda b_, k_, rhs_: _dot(
        jnp.concatenate([b_, k_], axis=0).T.astype(BF16), rhs_), bl2, kl2, rhs)
    bot = _each(lambda b_, k_, rhs_: _dot(
        jnp.concatenate([b_, k_], axis=1).astype(BF16), rhs_), a_rb, a_rk, rhs)

    p_mat = _each(lambda gl, top_: (jnp.where(eye, gl, 0.0) + top_[:, :LANES]).astype(BF16),
                  g_last, top)
    q_mat = _each(lambda top_: top_[:, LANES:], top)
    g_mat = _each(lambda rt_, b_: (rt_ + b_[:CHUNK, :LANES] + b_[CHUNK:, :LANES]).astype(BF16),
                  rt, bot)
    y0 = _each(lambda b_: b_[:CHUNK, LANES:] + b_[CHUNK:, LANES:], bot)
    return p_mat, q_mat, g_mat, y0


def _rwkv_kernel(z_ref, zprev_ref, mu_ref, w0_ref, a0_ref, lora_ref, gl_ref, kk_ref, ka_ref,
                 rk_ref, lng_ref, lnb_ref, seg_ref, tril_ref, o_ref,
                 st_ref, r_s, lw_s, k_s, v_s, kk_s, as_s, y_s, g_s, bonus_s, *, width, tb_rows):
    tb = pl.program_id(1)

    @pl.when(tb == 0)
    def _():
        st_ref[...] = jnp.zeros_like(st_ref)

    z = z_ref[...]
    prev_row = jnp.where(tb == 0, 0.0, zprev_ref[7:8, :])
    row = lax.broadcasted_iota(jnp.int32, z.shape, 0)
    z_shift = jnp.where(row == 0, prev_row, pltpu.roll(z, 1, 0))
    zs = z + (z_shift - z) * mu_ref[...]
    r = zs[:, 0:width]
    k = zs[:, width:2 * width]
    v = zs[:, 2 * width:3 * width]
    lora_in = zs[:, 3 * width:3 * width + DECAY_LORA + AAA_LORA]
    gd = zs[:, 3 * width + DECAY_LORA + AAA_LORA:]
    lane = lax.broadcasted_iota(jnp.int32, lora_in.shape, 1)
    lora_in = jnp.where(lane < DECAY_LORA, jnp.tanh(lora_in), lora_in)
    lora = _dot(lora_in.astype(BF16), lora_ref[...])
    wx = -(w0_ref[...] + lora[:, :width])
    softplus = jnp.maximum(wx, 0.0) + jnp.log(1.0 + jnp.exp(-jnp.abs(wx)))
    lw = -jnp.exp(-softplus - 0.5)
    asig = jax.nn.sigmoid(a0_ref[...] + lora[:, width:])
    gate = _dot(jax.nn.sigmoid(gd).astype(BF16), gl_ref[...])
    seg = seg_ref[...]
    kk = k * kk_ref[...]
    kk = kk / jnp.maximum(jnp.sqrt(_split_dot(kk * kk, seg)), 1e-12)
    k2 = k * (1.0 + (asig - 1.0) * ka_ref[...])
    bonus = _split_dot(r * k2 * rk_ref[...], seg) * v
    r_s[...] = r
    lw_s[...] = lw
    k_s[...] = k2
    v_s[...] = v
    kk_s[...] = kk
    as_s[...] = asig
    g_s[...] = gate
    bonus_s[...] = bonus

    ri = lax.broadcasted_iota(jnp.int32, (PAIR, PAIR), 0)
    ci = lax.broadcasted_iota(jnp.int32, (PAIR, PAIR), 1)
    lane_row = lax.broadcasted_iota(jnp.int32, (1, LANES), 1)
    m0 = (lane_row < HEAD_SIZE).astype(F32)
    consts = (tril_ref[...], m0, 1.0 - m0, ri > ci, ri >= ci, ri == ci)
    n_pairs = width // LANES

    def chunk_body(ci_, carry):
        rows = [pl.ds(pl.multiple_of((ci_ * CHUNKS_PER_ITER + u) * CHUNK, CHUNK), CHUNK)
                for u in range(CHUNKS_PER_ITER)]
        lanes = [slice(p * LANES, (p + 1) * LANES) for p in range(n_pairs)]
        load = lambda ref: [ref[rw, ln] for rw in rows for ln in lanes]
        p_mat, q_mat, g_mat, y0 = _wkv_chunk(load(r_s), load(lw_s), load(k_s), load(v_s),
                                             load(kk_s), load(as_s), consts)
        h = [st_ref[p] for p in range(n_pairs)]
        for u in range(CHUNKS_PER_ITER):
            for p in range(n_pairs):
                i = u * n_pairs + p
                h_b = h[p].astype(BF16)
                y_s[rows[u], lanes[p]] = _dot(g_mat[i], h_b) + y0[i]
                h[p] = _dot(p_mat[i], h_b) + q_mat[i]
        for p in range(n_pairs):
            st_ref[p] = h[p]
        return carry

    lax.fori_loop(0, tb_rows // (CHUNK * CHUNKS_PER_ITER), chunk_body, 0)

    y = y_s[...]
    inv_n = 1.0 / HEAD_SIZE
    mean = _split_dot(y, seg) * inv_n
    yc = y - mean
    var = _split_dot(yc * yc, seg) * inv_n
    yn = yc * lax.rsqrt(var + GN_EPS) * lng_ref[...] + lnb_ref[...]
    o_ref[...] = (yn + bonus_s[...]) * g_s[...]


def _pool(cur, prev, pos, pw_ref):
    ext = jnp.concatenate([prev, cur], axis=0)
    outs = []
    for j, win in enumerate(POOL_WINDOWS):
        lanes = slice(j * LANES, (j + 1) * LANES)
        s = ext[:, lanes]
        shift = 1
        while shift < win:
            s = s + pltpu.roll(s, shift, 0)
            shift *= 2
        d = s[POOL_HALO:, :] / jnp.minimum(pos, float(win)) - cur[:, lanes]
        outs.append(_dot(d.astype(BF16), pw_ref[j]))
    return jnp.concatenate(outs, axis=1)


def _post_kernel(h_ref, ya_ref, u_ref, uprev_ref, p_ref, pw_ref, ps_ref, wo_ref, g1_ref, b1_ref,
                 wg_ref, wu_ref, wd_ref, wpg_ref, wpp_ref, g2_ref, b2_ref, o_ref,
                 *, alpha, width, tb_rows, n_split):
    tb = pl.program_id(1)
    sub = tb_rows // n_split
    h1s = []
    for s in range(n_split):
        rows = slice(s * sub, (s + 1) * sub)
        if s == 0:
            prev = jnp.where(tb == 0, 0.0, uprev_ref[...])
        else:
            prev = u_ref[s * sub - POOL_HALO:s * sub, :]
        pos = (tb * tb_rows + s * sub + 1
               + lax.broadcasted_iota(jnp.int32, (sub, 1), 0)).astype(F32)
        y_b = _pool(u_ref[rows, :], prev, pos, pw_ref) * ps_ref[...]
        mix = (_dot(ya_ref[rows, :].astype(BF16), wo_ref[:width, :])
               + _dot(y_b.astype(BF16), wo_ref[width:, :]))
        h1s.append(_layer_norm(alpha * h_ref[rows, :] + mix, g1_ref[...], b1_ref[...]))
    for s, h1 in enumerate(h1s):
        rows = slice(s * sub, (s + 1) * sub)
        hb = h1.astype(BF16)
        emb = (jax.nn.sigmoid(_dot(hb, wpg_ref[...]))
               * _dot(p_ref[rows, :].astype(BF16), wpp_ref[...]))
        gate = _dot(hb, wg_ref[...])
        act = gate * jax.nn.sigmoid(gate) * _dot(hb, wu_ref[...])
        ffn = _dot(act.astype(BF16), wd_ref[...])
        o_ref[rows, :] = _layer_norm(alpha * h1 + ffn + emb, g2_ref[...], b2_ref[...])


def _full(shape):
    return pl.BlockSpec(shape, lambda *_: (0,) * len(shape))


def _resident(shape):
    return pl.BlockSpec(shape, lambda *_: (0,) * len(shape), pipeline_mode=pl.Buffered(1))


def _params(semantics):
    return pltpu.CompilerParams(dimension_semantics=semantics, vmem_limit_bytes=VMEM_LIMIT)


def _layer(h_in, p_l, ln_in, prm, *, batch, seq):
    n_tok, d_model = h_in.shape
    width = prm['w0'].shape[-1]
    rwkv_in = 3 * width + DECAY_LORA + AAA_LORA + GATE_LORA
    depth = prm['depth']
    alpha = (2.0 * depth) ** 0.25

    tm = 512
    row = lambda c: pl.BlockSpec((tm, c), lambda i: (i, 0))
    w_in = prm['w_in'].astype(BF16)
    h, z_r, u = pl.pallas_call(
        functools.partial(_proj_in_kernel, rwkv_in=rwkv_in),
        grid=(n_tok // tm,),
        in_specs=[row(d_model), _full((1, d_model)), _full((1, d_model)), _full(w_in.shape)],
        out_specs=[row(d_model), row(rwkv_in), row(width)],
        out_shape=[jax.ShapeDtypeStruct((n_tok, d_model), F32),
                   jax.ShapeDtypeStruct((n_tok, rwkv_in), F32),
                   jax.ShapeDtypeStruct((n_tok, width), F32)],
        compiler_params=_params(("parallel",)),
        name="proj_in",
    )(h_in, ln_in[0].reshape(1, -1), ln_in[1].reshape(1, -1), w_in)

    tb_rows = 512
    n_tb = seq // tb_rows
    lora_w = jnp.zeros((DECAY_LORA + AAA_LORA, 2 * width), F32)
    lora_w = lora_w.at[:DECAY_LORA, :width].set(prm['wl_up'])
    lora_w = lora_w.at[DECAY_LORA:, width:].set(prm['al_up']).astype(BF16)
    head_id = jnp.arange(width) // HEAD_SIZE
    seg = (head_id[:, None] == head_id[None, :]).astype(BF16)
    tril = jnp.tril(jnp.ones((CHUNK, CHUNK), BF16))
    vec = lambda a: a.reshape(1, -1)
    tok = lambda c: pl.BlockSpec((tb_rows, c), lambda b, t: (b * n_tb + t, 0))
    prev8 = pl.BlockSpec(
        (8, rwkv_in), lambda b, t: (jnp.maximum((b * n_tb + t) * (tb_rows // 8) - 1, 0), 0))
    y_a = pl.pallas_call(
        functools.partial(_rwkv_kernel, width=width, tb_rows=tb_rows),
        grid=(batch, n_tb),
        in_specs=[tok(rwkv_in), prev8, _full((1, rwkv_in)), _full((1, width)), _full((1, width)),
                  _full(lora_w.shape), _full((GATE_LORA, width)), _full((1, width)),
                  _full((1, width)), _full((1, width)), _full((1, width)), _full((1, width)),
                  _full(seg.shape), _full(tril.shape)],
        out_specs=tok(width),
        out_shape=jax.ShapeDtypeStruct((n_tok, width), F32),
        scratch_shapes=[pltpu.VMEM((width // LANES, LANES, LANES), F32)]
        + [pltpu.VMEM((tb_rows, width), F32)] * 9,
        compiler_params=_params(("arbitrary", "arbitrary")),
        name="rwkv",
    )(z_r, z_r, vec(prm['mu_shift']), vec(prm['w0']), vec(prm['a0']), lora_w,
      prm['gl_up'].astype(BF16), vec(prm['k_k']), vec(prm['k_a']), vec(prm['r_k']),
      vec(prm['lnx_g']), vec(prm['lnx_b']), seg, tril)

    tp = 512
    n_tp = seq // tp
    d_ple = p_l.shape[-1]
    blk = lambda c: pl.BlockSpec((tp, c), lambda b, t: (b * n_tp + t, 0))
    halo = pl.BlockSpec(
        (POOL_HALO, width),
        lambda b, t: (jnp.maximum((b * n_tp + t) * (tp // POOL_HALO) - 1, 0), 0))
    weights = [prm['pool_w'].astype(BF16), vec(prm['pool_scale']), prm['w_out'].astype(BF16),
               vec(prm['ln1_g']), vec(prm['ln1_b']), prm['ffn_gate'].astype(BF16),
               prm['ffn_up'].astype(BF16), prm['ffn_down'].astype(BF16),
               prm['pl_gate'].astype(BF16), prm['pl_proj'].astype(BF16),
               vec(prm['ln2_g']), vec(prm['ln2_b'])]
    out = pl.pallas_call(
        functools.partial(_post_kernel, alpha=alpha, width=width, tb_rows=tp, n_split=2),
        grid=(batch, n_tp),
        in_specs=[blk(d_model), blk(width), blk(width), halo, blk(d_ple)]
        + [_resident(w.shape) for w in weights],
        out_specs=blk(d_model),
        out_shape=jax.ShapeDtypeStruct((n_tok, d_model), F32),
        compiler_params=_params(("parallel", "parallel")),
        name="post",
    )(h, y_a, u, u, p_l, *weights)
    return out


def kernel(x, p, ln_in_g, ln_in_b, w_in, mu_shift, w0, wl_up, a0, al_up, gl_up, k_k, k_a, r_k,
           lnx_g, lnx_b, pool_w, pool_scale, w_out, ln1_g, ln1_b, ffn_gate, ffn_up, ffn_down,
           pl_proj, pl_gate, ln2_g, ln2_b):
    batch, seq, d_model = x.shape
    depth = p.shape[0]
    assert depth == 1, "the fused input norm assumes a single layer"
    stacked = dict(w_in=w_in, mu_shift=mu_shift, w0=w0, wl_up=wl_up, a0=a0, al_up=al_up,
                   gl_up=gl_up, k_k=k_k, k_a=k_a, r_k=r_k, lnx_g=lnx_g, lnx_b=lnx_b,
                   pool_w=pool_w, pool_scale=pool_scale, w_out=w_out, ln1_g=ln1_g, ln1_b=ln1_b,
                   ffn_gate=ffn_gate, ffn_up=ffn_up, ffn_down=ffn_down, pl_proj=pl_proj,
                   pl_gate=pl_gate, ln2_g=ln2_g, ln2_b=ln2_b)
    prm = {name: t[0] for name, t in stacked.items()}
    prm['depth'] = depth
    out = _layer(x.reshape(batch * seq, d_model), p[0].reshape(batch * seq, -1),
                 (ln_in_g, ln_in_b), prm, batch=batch, seq=seq)
    return out.reshape(batch, seq, d_model)
```

```python
import functools

import jax
import jax.numpy as jnp
from jax import lax
from jax.experimental import pallas as pl
from jax.experimental.pallas import tpu as pltpu

F32 = jnp.float32
BF16 = jnp.bfloat16

HEAD_SIZE = 64
POOL_WINDOWS = (2, 4, 8, 16)
DECAY_LORA = 64
AAA_LORA = 64
GATE_LORA = 128
LN_EPS = 1e-5
GN_EPS = 1e-5 * HEAD_SIZE

LANES = 128
CHUNK = 64
PAIR = 2 * CHUNK
CHUNKS_PER_ITER = 4
POOL_HALO = 16
VMEM_LIMIT = 56 * 1024 * 1024
FRONT_ROWS = 512
POST_ROWS = 512
PROJ_SPLITS = (768, 1792)
DECAY_SCALE = 0.6065306597126334


def _dot(a, b):
    return jnp.dot(a, b, preferred_element_type=F32)


def _dot_nt(a, b):
    return lax.dot_general(a, b, (((1,), (1,)), ((), ())), preferred_element_type=F32)


def _layer_norm(x, g, b):
    mu = jnp.mean(x, axis=-1, keepdims=True)
    xc = x - mu
    var = jnp.mean(xc * xc, axis=-1, keepdims=True)
    return xc * lax.rsqrt(var + LN_EPS) * g + b


def _each(fn, *lists):
    return [fn(*xs) for xs in zip(*lists)]


def _wkv_chunk(r, lw, k, v, kk, asig, c):
    tril, m0, m1, strict, incl, eye = c

    def cumsum(x):
        hi = x.astype(BF16)
        lo = (x - hi.astype(F32)).astype(BF16)
        both = _dot(tril, jnp.concatenate([hi, lo], axis=1))
        return both[:, :LANES] + both[:, LANES:]

    def stack(x):
        return jnp.concatenate([x * m0, x * m1], axis=0)

    cum = _each(cumsum, lw)
    g = _each(jnp.exp, cum)
    gprev = _each(lambda cu, x: jnp.exp(cu - x), cum, lw)
    ginv = _each(lambda cu: jnp.exp(-cu), cum)
    g_last = _each(lambda x: x[CHUNK - 1:CHUNK, :], g)
    rt = _each(lambda x, y: x * y, r, g)
    bt = _each(lambda x, y, z: x * y * z, kk, asig, ginv)
    kt = _each(lambda x, y: x * y, k, ginv)
    at2 = _each(lambda x, y: stack(-(x * y)), kk, gprev)
    rt2, bt2, kt2, v2 = _each(stack, rt), _each(stack, bt), _each(stack, kt), _each(stack, v)
    bl2 = _each(lambda x, y: stack(x * y), bt, g_last)
    kl2 = _each(lambda x, y: stack(x * y), kt, g_last)

    a_all = _each(lambda a_, r_, b_, k_: _dot_nt(
        jnp.concatenate([a_, r_], axis=0).astype(BF16),
        jnp.concatenate([b_, k_], axis=0).astype(BF16)), at2, rt2, bt2, kt2)
    a_ab = _each(lambda a_: jnp.where(strict, a_[:PAIR, :PAIR], 0.0), a_all)
    a_ak = _each(lambda a_: jnp.where(strict, a_[:PAIR, PAIR:], 0.0), a_all)
    a_rb = _each(lambda a_: jnp.where(incl, a_[PAIR:, :PAIR], 0.0), a_all)
    a_rk = _each(lambda a_: jnp.where(incl, a_[PAIR:, PAIR:], 0.0), a_all)

    t = _each(lambda n: jnp.where(eye, 1.0, 0.0) + n, a_ab)
    s = _each(lambda n: _dot(n.astype(BF16), n.astype(BF16)), a_ab)
    akv = _each(lambda a_, v_: _dot(a_.astype(BF16), v_.astype(BF16)), a_ak, v2)
    for _ in range(CHUNK.bit_length() - 3):
        ps = _each(lambda t_, s_: _dot(jnp.concatenate([t_, s_], axis=0).astype(BF16),
                                       s_.astype(BF16)), t, s)
        t = _each(lambda t_, ps_: t_ + ps_[:PAIR], t, ps)
        s = _each(lambda ps_: ps_[PAIR:], ps)
    t = _each(lambda t_, s_: t_ + _dot(t_.astype(BF16), s_.astype(BF16)), t, s)

    wu = _each(lambda t_, a_, akv_: _dot(
        t_.astype(BF16), jnp.concatenate([a_, akv_], axis=1).astype(BF16)), t, at2, akv)
    rhs = _each(lambda wu_, v_: jnp.concatenate(
        [wu_, jnp.concatenate([jnp.zeros_like(v_), v_], axis=1)], axis=0).astype(BF16), wu, v2)
    top = _each(lambda b_, k_, rhs_: _dot(
        jnp.concatenate([b_, k_], axis=0).T.astype(BF16), rhs_), bl2, kl2, rhs)
    bot = _each(lambda b_, k_, rhs_: _dot(
        jnp.concatenate([b_, k_], axis=1).astype(BF16), rhs_), a_rb, a_rk, rhs)

    p_mat = _each(lambda gl, top_: (jnp.where(eye, gl, 0.0) + top_[:, :LANES]).astype(BF16),
                  g_last, top)
    q_mat = _each(lambda top_: top_[:, LANES:], top)
    g_mat = _each(lambda rt_, b_: (rt_ + b_[:CHUNK, :LANES] + b_[CHUNK:, :LANES]).astype(BF16),
                  rt, bot)
    y0 = _each(lambda b_: b_[:CHUNK, LANES:] + b_[CHUNK:, LANES:], bot)
    return p_mat, q_mat, g_mat, y0


def _front_kernel(x_ref, ing_ref, inb_ref, win_ref, mu_ref, w0_ref, a0_ref, lora_ref, gl_ref,
                  kk_ref, ka_ref, rk_ref, gng_ref, gnb_ref, seg_ref, tril_ref, ya_ref, u_ref,
                  z_buf, z_last, st_ref, r_s, lw_s, k_s, v_s, kk_s, as_s, y_s, g_s, bonus_s,
                  *, width, tb_rows, n_tb):
    s = pl.program_id(0)
    rwkv_in = mu_ref.shape[-1]
    first = lax.rem(s + n_tb - 1, n_tb) == 0

    @pl.when(s == 0)
    def _():
        z_buf[...] = jnp.zeros_like(z_buf)
        z_last[...] = jnp.zeros_like(z_last)
        y_s[...] = jnp.zeros_like(y_s)

    @pl.when(jnp.logical_or(first, s == 0))
    def _():
        st_ref[...] = jnp.zeros_like(st_ref)

    z = z_buf[:, :rwkv_in]
    u_ref[...] = z_buf[:, rwkv_in:]
    prev_row = jnp.where(first, 0.0, z_last[7:8, :])
    z_last[...] = z[tb_rows - 8:, :]
    row = lax.broadcasted_iota(jnp.int32, z.shape, 0)
    z_shift = jnp.where(row == 0, prev_row, pltpu.roll(z, 1, 0))
    zs = z + (z_shift - z) * mu_ref[...]

    hb = _layer_norm(x_ref[...], ing_ref[...], inb_ref[...]).astype(BF16)

    def project(lo, hi):
        z_buf[:, lo:hi] = _dot(hb, win_ref[:, lo:hi])

    project(0, PROJ_SPLITS[0])
    r = zs[:, 0:width]
    k = zs[:, width:2 * width]
    v = zs[:, 2 * width:3 * width]
    lora_in = zs[:, 3 * width:3 * width + DECAY_LORA + AAA_LORA]
    gd = zs[:, 3 * width + DECAY_LORA + AAA_LORA:]
    lane = lax.broadcasted_iota(jnp.int32, lora_in.shape, 1)
    lora_in = jnp.where(lane < DECAY_LORA, jnp.tanh(lora_in), lora_in)
    lora = _dot(lora_in.astype(BF16), lora_ref[...])
    gate = _dot(jax.nn.sigmoid(gd).astype(BF16), gl_ref[...])
    seg = seg_ref[...]

    def head_sum(t):
        tb16 = t.astype(BF16)
        return jnp.concatenate([_dot(tb16[:, i:i + LANES], seg) for i in range(0, width, LANES)],
                               axis=1)

    kk = k * kk_ref[...]
    kk_ss = head_sum(kk * kk)
    project(PROJ_SPLITS[0], PROJ_SPLITS[1])

    lw = -DECAY_SCALE * jax.nn.sigmoid(w0_ref[...] + lora[:, :width])
    asig = jax.nn.sigmoid(a0_ref[...] + lora[:, width:])
    kk = kk * lax.rsqrt(jnp.maximum(kk_ss, 1e-24))
    k2 = k * (1.0 + (asig - 1.0) * ka_ref[...])
    bonus = head_sum(r * k2 * rk_ref[...]) * v
    project(PROJ_SPLITS[1], rwkv_in + width)
    r_s[...] = r
    lw_s[...] = lw
    k_s[...] = k2
    v_s[...] = v
    kk_s[...] = kk
    as_s[...] = asig
    g_s[...] = gate
    bonus_s[...] = bonus

    ri = lax.broadcasted_iota(jnp.int32, (PAIR, PAIR), 0)
    ci = lax.broadcasted_iota(jnp.int32, (PAIR, PAIR), 1)
    lane_row = lax.broadcasted_iota(jnp.int32, (1, LANES), 1)
    m0 = (lane_row < HEAD_SIZE).astype(F32)
    consts = (tril_ref[...], m0, 1.0 - m0, ri > ci, ri >= ci, ri == ci)
    n_pairs = width // LANES

    def chunk_body(ci_, carry):
        rows = [pl.ds(pl.multiple_of((ci_ * CHUNKS_PER_ITER + u) * CHUNK, CHUNK), CHUNK)
                for u in range(CHUNKS_PER_ITER)]
        lanes = [slice(p * LANES, (p + 1) * LANES) for p in range(n_pairs)]
        load = lambda ref: [ref[rw, ln] for rw in rows for ln in lanes]
        p_mat, q_mat, g_mat, y0 = _wkv_chunk(load(r_s), load(lw_s), load(k_s), load(v_s),
                                             load(kk_s), load(as_s), consts)
        h_st = [st_ref[p] for p in range(n_pairs)]
        for u in range(CHUNKS_PER_ITER):
            for p in range(n_pairs):
                i = u * n_pairs + p
                h_b = h_st[p].astype(BF16)
                y_s[rows[u], lanes[p]] = _dot(g_mat[i], h_b) + y0[i]
                h_st[p] = _dot(p_mat[i], h_b) + q_mat[i]
        for p in range(n_pairs):
            st_ref[p] = h_st[p]
        return carry

    n_iter = jnp.where(s == 0, 0, tb_rows // (CHUNK * CHUNKS_PER_ITER))
    lax.fori_loop(0, n_iter, chunk_body, 0)

    y = y_s[...]
    inv_n = 1.0 / HEAD_SIZE
    mean = head_sum(y) * inv_n
    yc = y - mean
    var = head_sum(yc * yc) * inv_n
    yn = yc * lax.rsqrt(var + GN_EPS) * gng_ref[...] + gnb_ref[...]
    ya_ref[...] = (yn + bonus_s[...]) * g_s[...]


def _pool(cur, prev, pos, pw_ref):
    ext = jnp.concatenate([prev, cur], axis=0)
    outs = []
    for j, win in enumerate(POOL_WINDOWS):
        lanes = slice(j * LANES, (j + 1) * LANES)
        s = ext[:, lanes]
        shift = 1
        while shift < win:
            s = s + pltpu.roll(s, shift, 0)
            shift *= 2
        d = s[POOL_HALO:, :] / jnp.minimum(pos, float(win)) - cur[:, lanes]
        outs.append(_dot(d.astype(BF16), pw_ref[j]))
    return jnp.concatenate(outs, axis=1)


def _post_kernel(x_ref, ya_ref, u_ref, uprev_ref, p_ref, ing_ref, inb_ref, pw_ref, ps_ref, wo_ref,
                 g1_ref, b1_ref, wg_ref, wu_ref, wd_ref, wpg_ref, wpp_ref, g2_ref, b2_ref, o_ref,
                 *, alpha, width, tb_rows, n_split):
    tb = pl.program_id(1)
    sub = tb_rows // n_split
    h1s = []
    for s in range(n_split):
        rows = slice(s * sub, (s + 1) * sub)
        if s == 0:
            prev = jnp.where(tb == 0, 0.0, uprev_ref[...])
        else:
            prev = u_ref[s * sub - POOL_HALO:s * sub, :]
        pos = (tb * tb_rows + s * sub + 1
               + lax.broadcasted_iota(jnp.int32, (sub, 1), 0)).astype(F32)
        y_b = _pool(u_ref[rows, :], prev, pos, pw_ref) * ps_ref[...]
        mix = (_dot(ya_ref[rows, :].astype(BF16), wo_ref[:width, :])
               + _dot(y_b.astype(BF16), wo_ref[width:, :]))
        h = _layer_norm(x_ref[rows, :], ing_ref[...], inb_ref[...])
        h1s.append(_layer_norm(alpha * h + mix, g1_ref[...], b1_ref[...]))
    for s, h1 in enumerate(h1s):
        rows = slice(s * sub, (s + 1) * sub)
        hb = h1.astype(BF16)
        emb = (jax.nn.sigmoid(_dot(hb, wpg_ref[...]))
               * _dot(p_ref[rows, :].astype(BF16), wpp_ref[...]))
        gate = _dot(hb, wg_ref[...])
        act = gate * jax.nn.sigmoid(gate) * _dot(hb, wu_ref[...])
        ffn = _dot(act.astype(BF16), wd_ref[...])
        o_ref[rows, :] = _layer_norm(alpha * h1 + ffn + emb, g2_ref[...], b2_ref[...])


def _resident(shape):
    return pl.BlockSpec(shape, lambda *_: (0,) * len(shape), pipeline_mode=pl.Buffered(1))


def _params(semantics):
    return pltpu.CompilerParams(dimension_semantics=semantics, vmem_limit_bytes=VMEM_LIMIT)


def _layer(x, p_l, ln_in, prm, *, batch, seq):
    n_tok, d_model = x.shape
    width = prm['w0'].shape[-1]
    rwkv_in = 3 * width + DECAY_LORA + AAA_LORA + GATE_LORA
    alpha = (2.0 * prm['depth']) ** 0.25

    tb_rows = FRONT_ROWS
    n_tb = seq // tb_rows
    n_blocks = n_tok // tb_rows
    lora_w = jnp.zeros((DECAY_LORA + AAA_LORA, 2 * width), F32)
    lora_w = lora_w.at[:DECAY_LORA, :width].set(prm['wl_up'])
    lora_w = lora_w.at[DECAY_LORA:, width:].set(prm['al_up']).astype(BF16)
    head_id = jnp.arange(LANES) // HEAD_SIZE
    seg = (head_id[:, None] == head_id[None, :]).astype(BF16)
    tril = jnp.tril(jnp.ones((CHUNK, CHUNK), BF16))
    vec = lambda a: a.reshape(1, -1)
    ln_in = [vec(ln_in[0]), vec(ln_in[1])]
    front_params = ln_in + [prm['w_in'].astype(BF16), vec(prm['mu_shift']), vec(prm['w0']),
                            vec(prm['a0']), lora_w, prm['gl_up'].astype(BF16), vec(prm['k_k']),
                            vec(prm['k_a']), vec(prm['r_k']), vec(prm['lnx_g']),
                            vec(prm['lnx_b']), seg, tril]
    done = lambda c: pl.BlockSpec((tb_rows, c), lambda s: (jnp.maximum(s - 1, 0), 0))
    y_a, u = pl.pallas_call(
        functools.partial(_front_kernel, width=width, tb_rows=tb_rows, n_tb=n_tb),
        grid=(n_blocks + 1,),
        in_specs=[pl.BlockSpec((tb_rows, d_model), lambda s: (jnp.minimum(s, n_blocks - 1), 0))]
        + [_resident(w.shape) for w in front_params],
        out_specs=[done(width), done(width)],
        out_shape=[jax.ShapeDtypeStruct((n_tok, width), F32)] * 2,
        scratch_shapes=[pltpu.VMEM((tb_rows, rwkv_in + width), F32),
                        pltpu.VMEM((8, rwkv_in), F32),
                        pltpu.VMEM((width // LANES, LANES, LANES), F32)]
        + [pltpu.VMEM((tb_rows, width), F32)] * 9,
        compiler_params=_params(("arbitrary",)),
        name="front",
    )(x, *front_params)

    tp = POST_ROWS
    n_tp = seq // tp
    d_ple = p_l.shape[-1]
    blk = lambda c: pl.BlockSpec((tp, c), lambda b, t: (b * n_tp + t, 0))
    halo = pl.BlockSpec(
        (POOL_HALO, width),
        lambda b, t: (jnp.maximum((b * n_tp + t) * (tp // POOL_HALO) - 1, 0), 0))
    weights = ln_in + [prm['pool_w'].astype(BF16), vec(prm['pool_scale']),
                       prm['w_out'].astype(BF16), vec(prm['ln1_g']), vec(prm['ln1_b']),
                       prm['ffn_gate'].astype(BF16), prm['ffn_up'].astype(BF16),
                       prm['ffn_down'].astype(BF16), prm['pl_gate'].astype(BF16),
                       prm['pl_proj'].astype(BF16), vec(prm['ln2_g']), vec(prm['ln2_b'])]
    out = pl.pallas_call(
        functools.partial(_post_kernel, alpha=alpha, width=width, tb_rows=tp, n_split=2),
        grid=(batch, n_tp),
        in_specs=[blk(d_model), blk(width), blk(width), halo, blk(d_ple)]
        + [_resident(w.shape) for w in weights],
        out_specs=blk(d_model),
        out_shape=jax.ShapeDtypeStruct((n_tok, d_model), F32),
        compiler_params=_params(("parallel", "parallel")),
        name="post",
    )(x, y_a, u, u, p_l, *weights)
    return out


def kernel(x, p, ln_in_g, ln_in_b, w_in, mu_shift, w0, wl_up, a0, al_up, gl_up, k_k, k_a, r_k,
           lnx_g, lnx_b, pool_w, pool_scale, w_out, ln1_g, ln1_b, ffn_gate, ffn_up, ffn_down,
           pl_proj, pl_gate, ln2_g, ln2_b):
    batch, seq, d_model = x.shape
    depth = p.shape[0]
    assert depth == 1, "the fused input norm assumes a single layer"
    stacked = dict(w_in=w_in, mu_shift=mu_shift, w0=w0, wl_up=wl_up, a0=a0, al_up=al_up,
                   gl_up=gl_up, k_k=k_k, k_a=k_a, r_k=r_k, lnx_g=lnx_g, lnx_b=lnx_b,
                   pool_w=pool_w, pool_scale=pool_scale, w_out=w_out, ln1_g=ln1_g, ln1_b=ln1_b,
                   ffn_gate=ffn_gate, ffn_up=ffn_up, ffn_down=ffn_down, pl_proj=pl_proj,
                   pl_gate=pl_gate, ln2_g=ln2_g, ln2_b=ln2_b)
    prm = {name: t[0] for name, t in stacked.items()}
    prm['depth'] = depth
    out = _layer(x.reshape(batch * seq, d_model), p[0].reshape(batch * seq, -1),
                 (ln_in_g, ln_in_b), prm, batch=batch, seq=seq)
    return out.reshape(batch, seq, d_model)
```

```python
import functools

import jax
import jax.numpy as jnp
from jax import lax
from jax.experimental import pallas as pl
from jax.experimental.pallas import tpu as pltpu

F32 = jnp.float32
BF16 = jnp.bfloat16

HEAD_SIZE = 64
POOL_WINDOWS = (2, 4, 8, 16)
DECAY_LORA = 64
AAA_LORA = 64
GATE_LORA = 128
LN_EPS = 1e-5
GN_EPS = 1e-5 * HEAD_SIZE

LANES = 128
CHUNK = 64
PAIR = 2 * CHUNK
QUAD = 4 * HEAD_SIZE
CHUNKS_PER_ITER = 8
POOL_HALO = 16
VMEM_LIMIT = 56 * 1024 * 1024
FRONT_ROWS = 512
POST_ROWS = 512
PROJ_SPLITS = (1280, 2048)
DECAY_SCALE = 0.6065306597126334


def _dot(a, b):
    return jnp.dot(a, b, preferred_element_type=F32)


def _dot_nt(a, b):
    return lax.dot_general(a, b, (((1,), (1,)), ((), ())), preferred_element_type=F32)


def _layer_norm(x, g, b):
    mu = jnp.mean(x, axis=-1, keepdims=True)
    xc = x - mu
    var = jnp.mean(xc * xc, axis=-1, keepdims=True)
    return xc * lax.rsqrt(var + LN_EPS) * g + b


def _each(fn, *lists):
    return [fn(*xs) for xs in zip(*lists)]


def _wkv_chunk(r, lw, k, v, kk, asig, c):
    tril, m0, m1, strict, incl, eye, eye_q, bd_q, bd_p = c

    def cumsum(x):
        hi = x.astype(BF16)
        lo = (x - hi.astype(F32)).astype(BF16)
        both = _dot(tril, jnp.concatenate([hi, lo], axis=1))
        return both[:, :QUAD] + both[:, QUAD:]

    def block_diag(x, mask):
        reps = mask.shape[0] // CHUNK
        return jnp.where(mask, jnp.concatenate([x] * reps, axis=0), 0.0).astype(BF16)

    cum = _each(cumsum, lw)
    g = _each(jnp.exp, cum)
    gprev = _each(lambda cu, x: jnp.exp(cu - x), cum, lw)
    ginv = _each(lambda cu: jnp.exp(-cu), cum)
    g_last = _each(lambda x: x[CHUNK - 1:CHUNK, :], g)
    at = _each(lambda x, y: -(x * y), kk, gprev)
    rt = _each(lambda x, y: x * y, r, g)
    bt = _each(lambda x, y, z: x * y * z, kk, asig, ginv)
    kt = _each(lambda x, y: x * y, k, ginv)
    bl = _each(lambda x, y: x * y, bt, g_last)
    kl = _each(lambda x, y: x * y, kt, g_last)

    a_all = _each(lambda a_, r_, b_, k_: _dot_nt(
        jnp.concatenate([a_, r_], axis=0).astype(BF16),
        jnp.concatenate([block_diag(b_, bd_q), block_diag(k_, bd_q)], axis=0)), at, rt, bt, kt)
    a_ab = _each(lambda a_: jnp.where(strict, a_[:CHUNK, :QUAD], 0.0), a_all)
    a_ak = _each(lambda a_: jnp.where(strict, a_[:CHUNK, QUAD:], 0.0), a_all)
    a_rb = _each(lambda a_: jnp.where(incl, a_[CHUNK:, :QUAD], 0.0), a_all)
    a_rk = _each(lambda a_: jnp.where(incl, a_[CHUNK:, QUAD:], 0.0), a_all)

    t = _each(lambda n: jnp.where(eye_q, 1.0, 0.0) + n, a_ab)
    s = _each(lambda n: _dot(n.astype(BF16), block_diag(n, bd_q)), a_ab)
    akv = _each(lambda a_, v_: _dot(a_.astype(BF16), block_diag(v_, bd_q)), a_ak, v)
    for _ in range(CHUNK.bit_length() - 3):
        ps = _each(lambda t_, s_: _dot(jnp.concatenate([t_, s_], axis=0).astype(BF16),
                                       block_diag(s_, bd_q)), t, s)
        t = _each(lambda t_, ps_: t_ + ps_[:CHUNK], t, ps)
        s = _each(lambda ps_: ps_[CHUNK:], ps)
    t = _each(lambda t_, s_: t_ + _dot(t_.astype(BF16), block_diag(s_, bd_q)), t, s)

    def pairs(tiles):
        return [x[:, i:i + LANES] for x in tiles for i in range(0, QUAD, LANES)]

    def stack(x):
        return jnp.concatenate([x * m0, x * m1], axis=0)

    t_bd = _each(lambda x: block_diag(x, bd_p), pairs(t))
    arb_bd = _each(lambda x: block_diag(x, bd_p), pairs(a_rb))
    ark_bd = _each(lambda x: block_diag(x, bd_p), pairs(a_rk))
    at2, akv2, v2 = _each(stack, pairs(at)), _each(stack, pairs(akv)), _each(stack, pairs(v))
    bl2, kl2 = _each(stack, pairs(bl)), _each(stack, pairs(kl))
    rt_p, gl_p = pairs(rt), pairs(g_last)

    wu = _each(lambda t_, a_, akv_: _dot(
        t_, jnp.concatenate([a_, akv_], axis=1).astype(BF16)), t_bd, at2, akv2)
    rhs = _each(lambda wu_, v_: jnp.concatenate(
        [wu_, jnp.concatenate([jnp.zeros_like(v_), v_], axis=1)], axis=0).astype(BF16), wu, v2)
    top = _each(lambda b_, k_, rhs_: _dot(
        jnp.concatenate([b_, k_], axis=0).T.astype(BF16), rhs_), bl2, kl2, rhs)
    bot = _each(lambda b_, k_, rhs_: _dot(jnp.concatenate([b_, k_], axis=1), rhs_),
                arb_bd, ark_bd, rhs)

    p_mat = _each(lambda gl, top_: (jnp.where(eye, gl, 0.0) + top_[:, :LANES]).astype(BF16),
                  gl_p, top)
    q_mat = _each(lambda top_: top_[:, LANES:], top)
    g_mat = _each(lambda rt_, b_: (rt_ + b_[:CHUNK, :LANES] + b_[CHUNK:, :LANES]).astype(BF16),
                  rt_p, bot)
    y0 = _each(lambda b_: b_[:CHUNK, LANES:] + b_[CHUNK:, LANES:], bot)
    return p_mat, q_mat, g_mat, y0


def _front_kernel(x_ref, ing_ref, inb_ref, win_ref, mu_ref, w0_ref, a0_ref, lora_ref, gl_ref,
                  kk_ref, ka_ref, rk_ref, gng_ref, gnb_ref, seg_ref, tril_ref, ya_ref, u_ref,
                  z_buf, z_last, st_ref, r_s, lw_s, k_s, v_s, kk_s, as_s, y_s, g_s, bonus_s,
                  *, width, tb_rows, n_tb):
    s = pl.program_id(0)
    rwkv_in = mu_ref.shape[-1]
    first = lax.rem(s + n_tb - 1, n_tb) == 0

    @pl.when(s == 0)
    def _():
        z_buf[...] = jnp.zeros_like(z_buf)
        z_last[...] = jnp.zeros_like(z_last)
        y_s[...] = jnp.zeros_like(y_s)

    @pl.when(jnp.logical_or(first, s == 0))
    def _():
        st_ref[...] = jnp.zeros_like(st_ref)

    z = z_buf[:, :rwkv_in]
    u_ref[...] = z_buf[:, rwkv_in:]
    prev_row = jnp.where(first, 0.0, z_last[7:8, :])
    z_last[...] = z[tb_rows - 8:, :]

    hb = _layer_norm(x_ref[...], ing_ref[...], inb_ref[...]).astype(BF16)

    def project(lo, hi):
        z_buf[:, lo:hi] = _dot(hb, win_ref[:, lo:hi])

    project(0, PROJ_SPLITS[0])
    row = lax.broadcasted_iota(jnp.int32, z.shape, 0)
    z_shift = jnp.where(row == 0, prev_row, pltpu.roll(z, 1, 0))
    zs = z + (z_shift - z) * mu_ref[...]
    r = zs[:, 0:width]
    k = zs[:, width:2 * width]
    v = zs[:, 2 * width:3 * width]
    lora_in = zs[:, 3 * width:3 * width + DECAY_LORA + AAA_LORA]
    gd = zs[:, 3 * width + DECAY_LORA + AAA_LORA:]
    lane = lax.broadcasted_iota(jnp.int32, lora_in.shape, 1)
    lora_in = jnp.where(lane < DECAY_LORA, jnp.tanh(lora_in), lora_in)
    lora = _dot(lora_in.astype(BF16), lora_ref[...])
    gate = _dot(jax.nn.sigmoid(gd).astype(BF16), gl_ref[...])
    seg = seg_ref[...]

    def head_sum(t):
        tb16 = t.astype(BF16)
        return jnp.concatenate([_dot(tb16[:, i:i + LANES], seg) for i in range(0, width, LANES)],
                               axis=1)

    kk = k * kk_ref[...]
    kk_ss = head_sum(kk * kk)
    project(PROJ_SPLITS[0], PROJ_SPLITS[1])

    lw = -DECAY_SCALE * jax.nn.sigmoid(w0_ref[...] + lora[:, :width])
    asig = jax.nn.sigmoid(a0_ref[...] + lora[:, width:])
    kk = kk * lax.rsqrt(jnp.maximum(kk_ss, 1e-24))
    k2 = k * (1.0 + (asig - 1.0) * ka_ref[...])
    bonus = head_sum(r * k2 * rk_ref[...]) * v
    project(PROJ_SPLITS[1], rwkv_in + width)
    r_s[...] = r
    lw_s[...] = lw
    k_s[...] = k2
    v_s[...] = v
    kk_s[...] = kk
    as_s[...] = asig
    g_s[...] = gate
    bonus_s[...] = bonus

    def iota(shape, dim):
        return lax.broadcasted_iota(jnp.int32, shape, dim)

    def head(idx):
        return lax.shift_right_logical(idx, HEAD_SIZE.bit_length() - 1)

    tok = iota((CHUNK, QUAD), 0)
    src_tok = iota((CHUNK, QUAD), 1) & (HEAD_SIZE - 1)
    m0 = (iota((1, LANES), 1) < HEAD_SIZE).astype(F32)
    consts = (tril_ref[...], m0, 1.0 - m0, tok > src_tok, tok >= src_tok,
              iota((PAIR, PAIR), 0) == iota((PAIR, PAIR), 1), tok == src_tok,
              head(iota((QUAD, QUAD), 0)) == head(iota((QUAD, QUAD), 1)),
              head(iota((PAIR, PAIR), 0)) == head(iota((PAIR, PAIR), 1)))
    n_pairs = width // LANES

    def chunk_body(ci_, carry):
        rows = [pl.ds(pl.multiple_of((ci_ * CHUNKS_PER_ITER + u) * CHUNK, CHUNK), CHUNK)
                for u in range(CHUNKS_PER_ITER)]
        lanes = [slice(p * LANES, (p + 1) * LANES) for p in range(n_pairs)]
        quads = [slice(q * QUAD, (q + 1) * QUAD) for q in range(width // QUAD)]
        load = lambda ref: [ref[rw, ln] for rw in rows for ln in quads]
        p_mat, q_mat, g_mat, y0 = _wkv_chunk(load(r_s), load(lw_s), load(k_s), load(v_s),
                                             load(kk_s), load(as_s), consts)
        h_st = [st_ref[p] for p in range(n_pairs)]
        for u in range(CHUNKS_PER_ITER):
            for p in range(n_pairs):
                i = u * n_pairs + p
                h_b = h_st[p].astype(BF16)
                y_s[rows[u], lanes[p]] = _dot(g_mat[i], h_b) + y0[i]
                h_st[p] = _dot(p_mat[i], h_b) + q_mat[i]
        for p in range(n_pairs):
            st_ref[p] = h_st[p]
        return carry

    n_iter = jnp.where(s == 0, 0, tb_rows // (CHUNK * CHUNKS_PER_ITER))
    lax.fori_loop(0, n_iter, chunk_body, 0)

    y = y_s[...]
    inv_n = 1.0 / HEAD_SIZE
    mean = head_sum(y) * inv_n
    yc = y - mean
    var = head_sum(yc * yc) * inv_n
    yn = yc * lax.rsqrt(var + GN_EPS) * gng_ref[...] + gnb_ref[...]
    ya_ref[...] = (yn + bonus_s[...]) * g_s[...]


def _pool(cur, prev, pos, pw_ref):
    ext = jnp.concatenate([prev, cur], axis=0)
    outs = []
    for j, win in enumerate(POOL_WINDOWS):
        lanes = slice(j * LANES, (j + 1) * LANES)
        s = ext[:, lanes]
        shift = 1
        while shift < win:
            s = s + pltpu.roll(s, shift, 0)
            shift *= 2
        d = s[POOL_HALO:, :] / jnp.minimum(pos, float(win)) - cur[:, lanes]
        outs.append(_dot(d.astype(BF16), pw_ref[j]))
    return jnp.concatenate(outs, axis=1)


def _post_kernel(x_ref, ya_ref, u_ref, uprev_ref, p_ref, ing_ref, inb_ref, pw_ref, ps_ref, wo_ref,
                 g1_ref, b1_ref, wg_ref, wu_ref, wd_ref, wpg_ref, wpp_ref, g2_ref, b2_ref, o_ref,
                 *, alpha, width, tb_rows, n_split):
    tb = pl.program_id(1)
    sub = tb_rows // n_split
    h1s = []
    for s in range(n_split):
        rows = slice(s * sub, (s + 1) * sub)
        if s == 0:
            prev = jnp.where(tb == 0, 0.0, uprev_ref[...])
        else:
            prev = u_ref[s * sub - POOL_HALO:s * sub, :]
        pos = (tb * tb_rows + s * sub + 1
               + lax.broadcasted_iota(jnp.int32, (sub, 1), 0)).astype(F32)
        y_b = _pool(u_ref[rows, :], prev, pos, pw_ref) * ps_ref[...]
        mix = (_dot(ya_ref[rows, :].astype(BF16), wo_ref[:width, :])
               + _dot(y_b.astype(BF16), wo_ref[width:, :]))
        h = _layer_norm(x_ref[rows, :], ing_ref[...], inb_ref[...])
        h1s.append(_layer_norm(alpha * h + mix, g1_ref[...], b1_ref[...]))
    for s, h1 in enumerate(h1s):
        rows = slice(s * sub, (s + 1) * sub)
        hb = h1.astype(BF16)
        emb = (jax.nn.sigmoid(_dot(hb, wpg_ref[...]))
               * _dot(p_ref[rows, :].astype(BF16), wpp_ref[...]))
        gate = _dot(hb, wg_ref[...])
        act = gate * jax.nn.sigmoid(gate) * _dot(hb, wu_ref[...])
        ffn = _dot(act.astype(BF16), wd_ref[...])
        o_ref[rows, :] = _layer_norm(alpha * h1 + ffn + emb, g2_ref[...], b2_ref[...])


def _resident(shape):
    return pl.BlockSpec(shape, lambda *_: (0,) * len(shape), pipeline_mode=pl.Buffered(1))


def _params(semantics):
    return pltpu.CompilerParams(dimension_semantics=semantics, vmem_limit_bytes=VMEM_LIMIT)


def _layer(x, p_l, ln_in, prm, *, batch, seq):
    n_tok, d_model = x.shape
    width = prm['w0'].shape[-1]
    rwkv_in = 3 * width + DECAY_LORA + AAA_LORA + GATE_LORA
    alpha = (2.0 * prm['depth']) ** 0.25

    tb_rows = FRONT_ROWS
    n_tb = seq // tb_rows
    n_blocks = n_tok // tb_rows
    lora_w = jnp.zeros((DECAY_LORA + AAA_LORA, 2 * width), F32)
    lora_w = lora_w.at[:DECAY_LORA, :width].set(prm['wl_up'])
    lora_w = lora_w.at[DECAY_LORA:, width:].set(prm['al_up']).astype(BF16)
    head_id = jnp.arange(LANES) // HEAD_SIZE
    seg = (head_id[:, None] == head_id[None, :]).astype(BF16)
    tril = jnp.tril(jnp.ones((CHUNK, CHUNK), BF16))
    vec = lambda a: a.reshape(1, -1)
    ln_in = [vec(ln_in[0]), vec(ln_in[1])]
    front_params = ln_in + [prm['w_in'].astype(BF16), vec(prm['mu_shift']), vec(prm['w0']),
                            vec(prm['a0']), lora_w, prm['gl_up'].astype(BF16), vec(prm['k_k']),
                            vec(prm['k_a']), vec(prm['r_k']), vec(prm['lnx_g']),
                            vec(prm['lnx_b']), seg, tril]
    done = lambda c: pl.BlockSpec((tb_rows, c), lambda s: (jnp.maximum(s - 1, 0), 0))
    y_a, u = pl.pallas_call(
        functools.partial(_front_kernel, width=width, tb_rows=tb_rows, n_tb=n_tb),
        grid=(n_blocks + 1,),
        in_specs=[pl.BlockSpec((tb_rows, d_model), lambda s: (jnp.minimum(s, n_blocks - 1), 0))]
        + [_resident(w.shape) for w in front_params],
        out_specs=[done(width), done(width)],
        out_shape=[jax.ShapeDtypeStruct((n_tok, width), F32)] * 2,
        scratch_shapes=[pltpu.VMEM((tb_rows, rwkv_in + width), F32),
                        pltpu.VMEM((8, rwkv_in), F32),
                        pltpu.VMEM((width // LANES, LANES, LANES), F32)]
        + [pltpu.VMEM((tb_rows, width), F32)] * 9,
        compiler_params=_params(("arbitrary",)),
        name="front",
    )(x, *front_params)

    tp = POST_ROWS
    n_tp = seq // tp
    d_ple = p_l.shape[-1]
    blk = lambda c: pl.BlockSpec((tp, c), lambda b, t: (b * n_tp + t, 0))
    halo = pl.BlockSpec(
        (POOL_HALO, width),
        lambda b, t: (jnp.maximum((b * n_tp + t) * (tp // POOL_HALO) - 1, 0), 0))
    weights = ln_in + [prm['pool_w'].astype(BF16), vec(prm['pool_scale']),
                       prm['w_out'].astype(BF16), vec(prm['ln1_g']), vec(prm['ln1_b']),
                       prm['ffn_gate'].astype(BF16), prm['ffn_up'].astype(BF16),
                       prm['ffn_down'].astype(BF16), prm['pl_gate'].astype(BF16),
                       prm['pl_proj'].astype(BF16), vec(prm['ln2_g']), vec(prm['ln2_b'])]
    out = pl.pallas_call(
        functools.partial(_post_kernel, alpha=alpha, width=width, tb_rows=tp, n_split=2),
        grid=(batch, n_tp),
        in_specs=[blk(d_model), blk(width), blk(width), halo, blk(d_ple)]
        + [_resident(w.shape) for w in weights],
        out_specs=blk(d_model),
        out_shape=jax.ShapeDtypeStruct((n_tok, d_model), F32),
        compiler_params=_params(("parallel", "parallel")),
        name="post",
    )(x, y_a, u, u, p_l, *weights)
    return out


def kernel(x, p, ln_in_g, ln_in_b, w_in, mu_shift, w0, wl_up, a0, al_up, gl_up, k_k, k_a, r_k,
           lnx_g, lnx_b, pool_w, pool_scale, w_out, ln1_g, ln1_b, ffn_gate, ffn_up, ffn_down,
           pl_proj, pl_gate, ln2_g, ln2_b):
    batch, seq, d_model = x.shape
    depth = p.shape[0]
    assert depth == 1, "the fused input norm assumes a single layer"
    stacked = dict(w_in=w_in, mu_shift=mu_shift, w0=w0, wl_up=wl_up, a0=a0, al_up=al_up,
                   gl_up=gl_up, k_k=k_k, k_a=k_a, r_k=r_k, lnx_g=lnx_g, lnx_b=lnx_b,
                   pool_w=pool_w, pool_scale=pool_scale, w_out=w_out, ln1_g=ln1_g, ln1_b=ln1_b,
                   ffn_gate=ffn_gate, ffn_up=ffn_up, ffn_down=ffn_down, pl_proj=pl_proj,
                   pl_gate=pl_gate, ln2_g=ln2_g, ln2_b=ln2_b)
    prm = {name: t[0] for name, t in stacked.items()}
    prm['depth'] = depth
    out = _layer(x.reshape(batch * seq, d_model), p[0].reshape(batch * seq, -1),
                 (ln_in_g, ln_in_b), prm, batch=batch, seq=seq)
    return out.reshape(batch, seq, d_model)
```

```python
import functools

import jax
import jax.numpy as jnp
from jax import lax
from jax.experimental import pallas as pl
from jax.experimental.pallas import tpu as pltpu

F32 = jnp.float32
BF16 = jnp.bfloat16

HEAD_SIZE = 64
POOL_WINDOWS = (2, 4, 8, 16)
DECAY_LORA = 64
AAA_LORA = 64
GATE_LORA = 128
LN_EPS = 1e-5
GN_EPS = 1e-5 * HEAD_SIZE

LANES = 128
CHUNK = 64
PAIR = 2 * CHUNK
QUAD = 4 * HEAD_SIZE
CHUNKS_PER_ITER = 8
GROUP_CHUNKS = 2
GROUP_LAG = 3
POOL_HALO = 16
VMEM_LIMIT = 56 * 1024 * 1024
FRONT_ROWS = 512
POST_ROWS = 1024
POST_SUB_ROWS = 256
PROJ_SPLITS = (1280, 2048)
DECAY_SCALE = 0.6065306597126334


def _dot(a, b):
    return jnp.dot(a, b, preferred_element_type=F32)


def _dot_nt(a, b):
    return lax.dot_general(a, b, (((1,), (1,)), ((), ())), preferred_element_type=F32)


def _layer_norm(x, g, b):
    mu = jnp.mean(x, axis=-1, keepdims=True)
    xc = x - mu
    var = jnp.mean(xc * xc, axis=-1, keepdims=True)
    return xc * lax.rsqrt(var + LN_EPS) * g + b


def _each(fn, *lists):
    return [fn(*xs) for xs in zip(*lists)]


def _wkv_stages(r, lw, k, v, kk, asig, c):
    tril, m0, m1, strict, incl, eye, eye_q, bd_q, bd_p = c

    def cumsum(x):
        hi = x.astype(BF16)
        lo = (x - hi.astype(F32)).astype(BF16)
        both = _dot(tril, jnp.concatenate([hi, lo], axis=1))
        return both[:, :QUAD] + both[:, QUAD:]

    def block_diag(x, mask):
        reps = mask.shape[0] // CHUNK
        return jnp.where(mask, jnp.concatenate([x] * reps, axis=0), 0.0).astype(BF16)

    cum = _each(cumsum, lw)
    yield
    g = _each(jnp.exp, cum)
    gprev = _each(lambda cu, x: jnp.exp(cu - x), cum, lw)
    ginv = _each(lambda cu: jnp.exp(-cu), cum)
    g_last = _each(lambda x: x[CHUNK - 1:CHUNK, :], g)
    at = _each(lambda x, y: -(x * y), kk, gprev)
    rt = _each(lambda x, y: x * y, r, g)
    bt = _each(lambda x, y, z: x * y * z, kk, asig, ginv)
    kt = _each(lambda x, y: x * y, k, ginv)
    bl = _each(lambda x, y: x * y, bt, g_last)
    kl = _each(lambda x, y: x * y, kt, g_last)
    yield

    a_all = _each(lambda a_, r_, b_, k_: _dot_nt(
        jnp.concatenate([a_, r_], axis=0).astype(BF16),
        jnp.concatenate([block_diag(b_, bd_q), block_diag(k_, bd_q)], axis=0)), at, rt, bt, kt)
    yield
    a_ab = _each(lambda a_: jnp.where(strict, a_[:CHUNK, :QUAD], 0.0), a_all)
    a_ak = _each(lambda a_: jnp.where(strict, a_[:CHUNK, QUAD:], 0.0), a_all)
    a_rb = _each(lambda a_: jnp.where(incl, a_[CHUNK:, :QUAD], 0.0), a_all)
    a_rk = _each(lambda a_: jnp.where(incl, a_[CHUNK:, QUAD:], 0.0), a_all)

    t = _each(lambda n: jnp.where(eye_q, 1.0, 0.0) + n, a_ab)
    s = _each(lambda n: _dot(n.astype(BF16), block_diag(n, bd_q)), a_ab)
    akv = _each(lambda a_, v_: _dot(a_.astype(BF16), block_diag(v_, bd_q)), a_ak, v)
    yield
    for _ in range(CHUNK.bit_length() - 3):
        ps = _each(lambda t_, s_: _dot(jnp.concatenate([t_, s_], axis=0).astype(BF16),
                                       block_diag(s_, bd_q)), t, s)
        t = _each(lambda t_, ps_: t_ + ps_[:CHUNK], t, ps)
        s = _each(lambda ps_: ps_[CHUNK:], ps)
        yield
    t = _each(lambda t_, s_: t_ + _dot(t_.astype(BF16), block_diag(s_, bd_q)), t, s)
    yield

    def pairs(tiles):
        return [x[:, i:i + LANES] for x in tiles for i in range(0, QUAD, LANES)]

    def stack(x):
        return jnp.concatenate([x * m0, x * m1], axis=0)

    t_bd = _each(lambda x: block_diag(x, bd_p), pairs(t))
    arb_bd = _each(lambda x: block_diag(x, bd_p), pairs(a_rb))
    ark_bd = _each(lambda x: block_diag(x, bd_p), pairs(a_rk))
    at2, akv2, v2 = _each(stack, pairs(at)), _each(stack, pairs(akv)), _each(stack, pairs(v))
    bl2, kl2 = _each(stack, pairs(bl)), _each(stack, pairs(kl))
    rt_p, gl_p = pairs(rt), pairs(g_last)

    wu = _each(lambda t_, a_, akv_: _dot(
        t_, jnp.concatenate([a_, akv_], axis=1).astype(BF16)), t_bd, at2, akv2)
    yield
    rhs = _each(lambda wu_, v_: jnp.concatenate(
        [wu_, jnp.concatenate([jnp.zeros_like(v_), v_], axis=1)], axis=0).astype(BF16), wu, v2)
    top = _each(lambda b_, k_, rhs_: _dot(
        jnp.concatenate([b_, k_], axis=0).T.astype(BF16), rhs_), bl2, kl2, rhs)
    yield
    bot = _each(lambda b_, k_, rhs_: _dot(jnp.concatenate([b_, k_], axis=1), rhs_),
                arb_bd, ark_bd, rhs)

    p_mat = _each(lambda gl, top_: (jnp.where(eye, gl, 0.0) + top_[:, :LANES]).astype(BF16),
                  gl_p, top)
    q_mat = _each(lambda top_: top_[:, LANES:], top)
    g_mat = _each(lambda rt_, b_: (rt_ + b_[:CHUNK, :LANES] + b_[CHUNK:, :LANES]).astype(BF16),
                  rt_p, bot)
    y0 = _each(lambda b_: b_[:CHUNK, LANES:] + b_[CHUNK:, LANES:], bot)
    return p_mat, q_mat, g_mat, y0


def _front_kernel(x_ref, ing_ref, inb_ref, win_ref, mu_ref, w0_ref, a0_ref, lora_ref, gl_ref,
                  kk_ref, ka_ref, rk_ref, gng_ref, gnb_ref, seg_ref, tril_ref, ya_ref, u_ref,
                  z_buf, z_last, st_ref, r_s, lw_s, k_s, v_s, kk_s, as_s, y_s, g_s, bonus_s,
                  *, width, tb_rows, n_tb):
    s = pl.program_id(0)
    rwkv_in = mu_ref.shape[-1]
    first = lax.rem(s + n_tb - 1, n_tb) == 0

    @pl.when(s == 0)
    def _():
        z_buf[...] = jnp.zeros_like(z_buf)
        z_last[...] = jnp.zeros_like(z_last)
        y_s[...] = jnp.zeros_like(y_s)

    @pl.when(jnp.logical_or(first, s == 0))
    def _():
        st_ref[...] = jnp.zeros_like(st_ref)

    z = z_buf[:, :rwkv_in]
    u_ref[...] = z_buf[:, rwkv_in:]
    prev_row = jnp.where(first, 0.0, z_last[7:8, :])
    z_last[...] = z[tb_rows - 8:, :]

    hb = _layer_norm(x_ref[...], ing_ref[...], inb_ref[...]).astype(BF16)

    def project(lo, hi):
        z_buf[:, lo:hi] = _dot(hb, win_ref[:, lo:hi])

    project(0, PROJ_SPLITS[0])
    row = lax.broadcasted_iota(jnp.int32, z.shape, 0)
    z_shift = jnp.where(row == 0, prev_row, pltpu.roll(z, 1, 0))
    zs = z + (z_shift - z) * mu_ref[...]
    r = zs[:, 0:width]
    k = zs[:, width:2 * width]
    v = zs[:, 2 * width:3 * width]
    lora_in = zs[:, 3 * width:3 * width + DECAY_LORA + AAA_LORA]
    gd = zs[:, 3 * width + DECAY_LORA + AAA_LORA:]
    lane = lax.broadcasted_iota(jnp.int32, lora_in.shape, 1)
    lora_in = jnp.where(lane < DECAY_LORA, jnp.tanh(lora_in), lora_in)
    lora = _dot(lora_in.astype(BF16), lora_ref[...])
    gate = _dot(jax.nn.sigmoid(gd).astype(BF16), gl_ref[...])
    seg = seg_ref[...]

    def head_sum(t):
        tb16 = t.astype(BF16)
        return jnp.concatenate([_dot(tb16[:, i:i + LANES], seg) for i in range(0, width, LANES)],
                               axis=1)

    kk = k * kk_ref[...]
    kk_ss = head_sum(kk * kk)
    project(PROJ_SPLITS[0], PROJ_SPLITS[1])

    lw = -DECAY_SCALE * jax.nn.sigmoid(w0_ref[...] + lora[:, :width])
    asig = jax.nn.sigmoid(a0_ref[...] + lora[:, width:])
    kk = kk * lax.rsqrt(jnp.maximum(kk_ss, 1e-24))
    k2 = k * (1.0 + (asig - 1.0) * ka_ref[...])
    bonus = head_sum(r * k2 * rk_ref[...]) * v
    project(PROJ_SPLITS[1], rwkv_in + width)
    r_s[...] = r
    lw_s[...] = lw
    k_s[...] = k2
    v_s[...] = v
    kk_s[...] = kk
    as_s[...] = asig
    g_s[...] = gate
    bonus_s[...] = bonus

    def iota(shape, dim):
        return lax.broadcasted_iota(jnp.int32, shape, dim)

    def head(idx):
        return lax.shift_right_logical(idx, HEAD_SIZE.bit_length() - 1)

    tok = iota((CHUNK, QUAD), 0)
    src_tok = iota((CHUNK, QUAD), 1) & (HEAD_SIZE - 1)
    m0 = (iota((1, LANES), 1) < HEAD_SIZE).astype(F32)
    consts = (tril_ref[...], m0, 1.0 - m0, tok > src_tok, tok >= src_tok,
              iota((PAIR, PAIR), 0) == iota((PAIR, PAIR), 1), tok == src_tok,
              head(iota((QUAD, QUAD), 0)) == head(iota((QUAD, QUAD), 1)),
              head(iota((PAIR, PAIR), 0)) == head(iota((PAIR, PAIR), 1)))
    n_pairs = width // LANES

    def chunk_body(ci_, carry):
        lanes = [slice(p * LANES, (p + 1) * LANES) for p in range(n_pairs)]
        quads = [slice(q * QUAD, (q + 1) * QUAD) for q in range(width // QUAD)]
        h_st = [st_ref[p] for p in range(n_pairs)]

        def group(first_chunk):
            rows = [pl.ds(pl.multiple_of((ci_ * CHUNKS_PER_ITER + first_chunk + u) * CHUNK, CHUNK),
                          CHUNK) for u in range(GROUP_CHUNKS)]
            load = lambda ref: [ref[rw, ln] for rw in rows for ln in quads]
            p_mat, q_mat, g_mat, y0 = yield from _wkv_stages(
                load(r_s), load(lw_s), load(k_s), load(v_s), load(kk_s), load(as_s), consts)
            for u in range(GROUP_CHUNKS):
                yield
                for p in range(n_pairs):
                    i = u * n_pairs + p
                    h_b = h_st[p].astype(BF16)
                    y_s[rows[u], lanes[p]] = _dot(g_mat[i], h_b) + y0[i]
                    h_st[p] = _dot(p_mat[i], h_b) + q_mat[i]

        active = []
        pending = [group(c0) for c0 in range(0, CHUNKS_PER_ITER, GROUP_CHUNKS)]
        tick = 0
        while pending or active:
            if pending and tick % GROUP_LAG == 0:
                active.append(pending.pop(0))
            tick += 1
            for gen in list(active):
                if next(gen, StopIteration) is StopIteration:
                    active.remove(gen)
        for p in range(n_pairs):
            st_ref[p] = h_st[p]
        return carry

    n_iter = jnp.where(s == 0, 0, tb_rows // (CHUNK * CHUNKS_PER_ITER))
    lax.fori_loop(0, n_iter, chunk_body, 0)

    y = y_s[...]
    inv_n = 1.0 / HEAD_SIZE
    mean = head_sum(y) * inv_n
    yc = y - mean
    var = head_sum(yc * yc) * inv_n
    yn = yc * lax.rsqrt(var + GN_EPS) * gng_ref[...] + gnb_ref[...]
    ya_ref[...] = (yn + bonus_s[...]) * g_s[...]


def _pool(cur, prev, pos, pw_ref):
    ext = jnp.concatenate([prev, cur], axis=0)
    outs = []
    for j, win in enumerate(POOL_WINDOWS):
        lanes = slice(j * LANES, (j + 1) * LANES)
        s = ext[:, lanes]
        shift = 1
        while shift < win:
            s = s + pltpu.roll(s, shift, 0)
            shift *= 2
        d = s[POOL_HALO:, :] / jnp.minimum(pos, float(win)) - cur[:, lanes]
        outs.append(_dot(d.astype(BF16), pw_ref[j]))
    return jnp.concatenate(outs, axis=1)


def _post_kernel(x_ref, ya_ref, u_ref, uprev_ref, p_ref, ing_ref, inb_ref, pw_ref, ps_ref, wo_ref,
                 g1_ref, b1_ref, wg_ref, wu_ref, wd_ref, wpg_ref, wpp_ref, g2_ref, b2_ref, o_ref,
                 *, alpha, width, tb_rows, n_split):
    tb = pl.program_id(1)
    sub = tb_rows // n_split
    h1s = []
    for s in range(n_split):
        rows = slice(s * sub, (s + 1) * sub)
        if s == 0:
            prev = jnp.where(tb == 0, 0.0, uprev_ref[...])
        else:
            prev = u_ref[s * sub - POOL_HALO:s * sub, :]
        pos = (tb * tb_rows + s * sub + 1
               + lax.broadcasted_iota(jnp.int32, (sub, 1), 0)).astype(F32)
        y_b = _pool(u_ref[rows, :], prev, pos, pw_ref) * ps_ref[...]
        mix = (_dot(ya_ref[rows, :].astype(BF16), wo_ref[:width, :])
               + _dot(y_b.astype(BF16), wo_ref[width:, :]))
        h = _layer_norm(x_ref[rows, :], ing_ref[...], inb_ref[...])
        h1s.append(_layer_norm(alpha * h + mix, g1_ref[...], b1_ref[...]))
    for s, h1 in enumerate(h1s):
        rows = slice(s * sub, (s + 1) * sub)
        hb = h1.astype(BF16)
        emb = (jax.nn.sigmoid(_dot(hb, wpg_ref[...]))
               * _dot(p_ref[rows, :].astype(BF16), wpp_ref[...]))
        gate = _dot(hb, wg_ref[...])
        act = gate * jax.nn.sigmoid(gate) * _dot(hb, wu_ref[...])
        ffn = _dot(act.astype(BF16), wd_ref[...])
        o_ref[rows, :] = _layer_norm(alpha * h1 + ffn + emb, g2_ref[...], b2_ref[...])


def _resident(shape):
    return pl.BlockSpec(shape, lambda *_: (0,) * len(shape), pipeline_mode=pl.Buffered(1))


def _params(semantics):
    return pltpu.CompilerParams(dimension_semantics=semantics, vmem_limit_bytes=VMEM_LIMIT)


def _layer(x, p_l, ln_in, prm, *, batch, seq):
    n_tok, d_model = x.shape
    width = prm['w0'].shape[-1]
    rwkv_in = 3 * width + DECAY_LORA + AAA_LORA + GATE_LORA
    alpha = (2.0 * prm['depth']) ** 0.25

    tb_rows = FRONT_ROWS
    n_tb = seq // tb_rows
    n_blocks = n_tok // tb_rows
    lora_w = jnp.zeros((DECAY_LORA + AAA_LORA, 2 * width), F32)
    lora_w = lora_w.at[:DECAY_LORA, :width].set(prm['wl_up'])
    lora_w = lora_w.at[DECAY_LORA:, width:].set(prm['al_up']).astype(BF16)
    head_id = jnp.arange(LANES) // HEAD_SIZE
    seg = (head_id[:, None] == head_id[None, :]).astype(BF16)
    tril = jnp.tril(jnp.ones((CHUNK, CHUNK), BF16))
    vec = lambda a: a.reshape(1, -1)
    ln_in = [vec(ln_in[0]), vec(ln_in[1])]
    front_params = ln_in + [prm['w_in'].astype(BF16), vec(prm['mu_shift']), vec(prm['w0']),
                            vec(prm['a0']), lora_w, prm['gl_up'].astype(BF16), vec(prm['k_k']),
                            vec(prm['k_a']), vec(prm['r_k']), vec(prm['lnx_g']),
                            vec(prm['lnx_b']), seg, tril]
    done = lambda c: pl.BlockSpec((tb_rows, c), lambda s: (jnp.maximum(s - 1, 0), 0))
    y_a, u = pl.pallas_call(
        functools.partial(_front_kernel, width=width, tb_rows=tb_rows, n_tb=n_tb),
        grid=(n_blocks + 1,),
        in_specs=[pl.BlockSpec((tb_rows, d_model), lambda s: (jnp.minimum(s, n_blocks - 1), 0))]
        + [_resident(w.shape) for w in front_params],
        out_specs=[done(width), done(width)],
        out_shape=[jax.ShapeDtypeStruct((n_tok, width), F32)] * 2,
        scratch_shapes=[pltpu.VMEM((tb_rows, rwkv_in + width), F32),
                        pltpu.VMEM((8, rwkv_in), F32),
                        pltpu.VMEM((width // LANES, LANES, LANES), F32)]
        + [pltpu.VMEM((tb_rows, width), F32)] * 9,
        compiler_params=_params(("arbitrary",)),
        name="front",
    )(x, *front_params)

    tp = POST_ROWS
    n_tp = seq // tp
    d_ple = p_l.shape[-1]
    blk = lambda c: pl.BlockSpec((tp, c), lambda b, t: (b * n_tp + t, 0))
    halo = pl.BlockSpec(
        (POOL_HALO, width),
        lambda b, t: (jnp.maximum((b * n_tp + t) * (tp // POOL_HALO) - 1, 0), 0))
    weights = ln_in + [prm['pool_w'].astype(BF16), vec(prm['pool_scale']),
                       prm['w_out'].astype(BF16), vec(prm['ln1_g']), vec(prm['ln1_b']),
                       prm['ffn_gate'].astype(BF16), prm['ffn_up'].astype(BF16),
                       prm['ffn_down'].astype(BF16), prm['pl_gate'].astype(BF16),
                       prm['pl_proj'].astype(BF16), vec(prm['ln2_g']), vec(prm['ln2_b'])]
    out = pl.pallas_call(
        functools.partial(_post_kernel, alpha=alpha, width=width, tb_rows=tp, n_split=tp // POST_SUB_ROWS),
        grid=(batch, n_tp),
        in_specs=[blk(d_model), blk(width), blk(width), halo, blk(d_ple)]
        + [_resident(w.shape) for w in weights],
        out_specs=blk(d_model),
        out_shape=jax.ShapeDtypeStruct((n_tok, d_model), F32),
        compiler_params=_params(("parallel", "parallel")),
        name="post",
    )(x, y_a, u, u, p_l, *weights)
    return out


def kernel(x, p, ln_in_g, ln_in_b, w_in, mu_shift, w0, wl_up, a0, al_up, gl_up, k_k, k_a, r_k,
           lnx_g, lnx_b, pool_w, pool_scale, w_out, ln1_g, ln1_b, ffn_gate, ffn_up, ffn_down,
           pl_proj, pl_gate, ln2_g, ln2_b):
    batch, seq, d_model = x.shape
    depth = p.shape[0]
    assert depth == 1, "the fused input norm assumes a single layer"
    stacked = dict(w_in=w_in, mu_shift=mu_shift, w0=w0, wl_up=wl_up, a0=a0, al_up=al_up,
                   gl_up=gl_up, k_k=k_k, k_a=k_a, r_k=r_k, lnx_g=lnx_g, lnx_b=lnx_b,
                   pool_w=pool_w, pool_scale=pool_scale, w_out=w_out, ln1_g=ln1_g, ln1_b=ln1_b,
                   ffn_gate=ffn_gate, ffn_up=ffn_up, ffn_down=ffn_down, pl_proj=pl_proj,
                   pl_gate=pl_gate, ln2_g=ln2_g, ln2_b=ln2_b)
    prm = {name: t[0] for name, t in stacked.items()}
    prm['depth'] = depth
    out = _layer(x.reshape(batch * seq, d_model), p[0].reshape(batch * seq, -1),
                 (ln_in_g, ln_in_b), prm, batch=batch, seq=seq)
    return out.reshape(batch, seq, d_model)
```

```python
import functools

import jax
import jax.numpy as jnp
from jax import lax
from jax.experimental import pallas as pl
from jax.experimental.pallas import tpu as pltpu

F32 = jnp.float32
BF16 = jnp.bfloat16

HEAD_SIZE = 64
POOL_WINDOWS = (2, 4, 8, 16)
DECAY_LORA = 64
AAA_LORA = 64
GATE_LORA = 128
LN_EPS = 1e-5
GN_EPS = 1e-5 * HEAD_SIZE

LANES = 128
CHUNK = 64
PAIR = 2 * CHUNK
QUAD = 4 * HEAD_SIZE
CHUNKS_PER_ITER = 8
GROUP_CHUNKS = 2
GROUP_LAG = 3
POOL_HALO = 16
VMEM_LIMIT = 56 * 1024 * 1024
FRONT_ROWS = 512
POST_ROWS = 512
POST_SUB_ROWS = 256
PROJ_SPLITS = (1280, 2048)
DECAY_SCALE = 0.6065306597126334


def _dot(a, b):
    return jnp.dot(a, b, preferred_element_type=F32)


def _dot_nt(a, b):
    return lax.dot_general(a, b, (((1,), (1,)), ((), ())), preferred_element_type=F32)


def _layer_norm(x, g, b):
    mu = jnp.mean(x, axis=-1, keepdims=True)
    xc = x - mu
    var = jnp.mean(xc * xc, axis=-1, keepdims=True)
    return xc * lax.rsqrt(var + LN_EPS) * g + b


def _each(fn, *lists):
    return [fn(*xs) for xs in zip(*lists)]


def _wkv_stages(r, lw, k, v, kk, asig, c):
    tril, m0, m1, strict, incl, eye, eye_q, bd_q, bd_p = c

    def cumsum(x):
        hi = x.astype(BF16)
        lo = (x - hi.astype(F32)).astype(BF16)
        both = _dot(tril, jnp.concatenate([hi, lo], axis=1))
        return both[:, :QUAD] + both[:, QUAD:]

    def block_diag(x, mask):
        reps = mask.shape[0] // CHUNK
        return jnp.where(mask, jnp.concatenate([x] * reps, axis=0), 0.0).astype(BF16)

    cum = _each(cumsum, lw)
    yield
    g = _each(jnp.exp, cum)
    gprev = _each(lambda cu, x: jnp.exp(cu - x), cum, lw)
    ginv = _each(lambda cu: jnp.exp(-cu), cum)
    g_last = _each(lambda x: x[CHUNK - 1:CHUNK, :], g)
    at = _each(lambda x, y: -(x * y), kk, gprev)
    rt = _each(lambda x, y: x * y, r, g)
    bt = _each(lambda x, y, z: x * y * z, kk, asig, ginv)
    kt = _each(lambda x, y: x * y, k, ginv)
    bl = _each(lambda x, y: x * y, bt, g_last)
    kl = _each(lambda x, y: x * y, kt, g_last)
    yield

    a_all = _each(lambda a_, r_, b_, k_: _dot_nt(
        jnp.concatenate([a_, r_], axis=0).astype(BF16),
        jnp.concatenate([block_diag(b_, bd_q), block_diag(k_, bd_q)], axis=0)), at, rt, bt, kt)
    yield
    a_ab = _each(lambda a_: jnp.where(strict, a_[:CHUNK, :QUAD], 0.0), a_all)
    a_ak = _each(lambda a_: jnp.where(strict, a_[:CHUNK, QUAD:], 0.0), a_all)
    a_rb = _each(lambda a_: jnp.where(incl, a_[CHUNK:, :QUAD], 0.0), a_all)
    a_rk = _each(lambda a_: jnp.where(incl, a_[CHUNK:, QUAD:], 0.0), a_all)

    t = _each(lambda n: jnp.where(eye_q, 1.0, 0.0) + n, a_ab)
    s = _each(lambda n: _dot(n.astype(BF16), block_diag(n, bd_q)), a_ab)
    akv = _each(lambda a_, v_: _dot(a_.astype(BF16), block_diag(v_, bd_q)), a_ak, v)
    yield
    for _ in range(CHUNK.bit_length() - 3):
        ps = _each(lambda t_, s_: _dot(jnp.concatenate([t_, s_], axis=0).astype(BF16),
                                       block_diag(s_, bd_q)), t, s)
        t = _each(lambda t_, ps_: t_ + ps_[:CHUNK], t, ps)
        s = _each(lambda ps_: ps_[CHUNK:], ps)
        yield
    t = _each(lambda t_, s_: t_ + _dot(t_.astype(BF16), block_diag(s_, bd_q)), t, s)
    yield

    def pairs(tiles):
        return [x[:, i:i + LANES] for x in tiles for i in range(0, QUAD, LANES)]

    def stack(x):
        return jnp.concatenate([x * m0, x * m1], axis=0)

    t_bd = _each(lambda x: block_diag(x, bd_p), pairs(t))
    arb_bd = _each(lambda x: block_diag(x, bd_p), pairs(a_rb))
    ark_bd = _each(lambda x: block_diag(x, bd_p), pairs(a_rk))
    at2, akv2, v2 = _each(stack, pairs(at)), _each(stack, pairs(akv)), _each(stack, pairs(v))
    bl2, kl2 = _each(stack, pairs(bl)), _each(stack, pairs(kl))
    rt_p, gl_p = pairs(rt), pairs(g_last)

    wu = _each(lambda t_, a_, akv_: _dot(
        t_, jnp.concatenate([a_, akv_], axis=1).astype(BF16)), t_bd, at2, akv2)
    yield
    rhs = _each(lambda wu_, v_: jnp.concatenate(
        [wu_, jnp.concatenate([jnp.zeros_like(v_), v_], axis=1)], axis=0).astype(BF16), wu, v2)
    top = _each(lambda b_, k_, rhs_: _dot(
        jnp.concatenate([b_, k_], axis=0).T.astype(BF16), rhs_), bl2, kl2, rhs)
    yield
    bot = _each(lambda b_, k_, rhs_: _dot(jnp.concatenate([b_, k_], axis=1), rhs_),
                arb_bd, ark_bd, rhs)

    p_mat = _each(lambda gl, top_: (jnp.where(eye, gl, 0.0) + top_[:, :LANES]).astype(BF16),
                  gl_p, top)
    q_mat = _each(lambda top_: top_[:, LANES:], top)
    g_mat = _each(lambda rt_, b_: (rt_ + b_[:CHUNK, :LANES] + b_[CHUNK:, :LANES]).astype(BF16),
                  rt_p, bot)
    y0 = _each(lambda b_: b_[:CHUNK, LANES:] + b_[CHUNK:, LANES:], bot)
    return p_mat, q_mat, g_mat, y0


def _front_kernel(x_ref, ing_ref, inb_ref, win_ref, mu_ref, w0_ref, a0_ref, lora_ref, gl_ref,
                  kk_ref, ka_ref, rk_ref, gng_ref, gnb_ref, seg_ref, tril_ref, ya_ref, u_ref,
                  z_buf, z_last, st_ref, r_s, lw_s, k_s, v_s, kk_s, as_s, y_s, g_s, bonus_s,
                  *, width, tb_rows, n_tb):
    s = pl.program_id(0)
    rwkv_in = mu_ref.shape[-1]
    first = lax.rem(s + n_tb - 1, n_tb) == 0

    @pl.when(s == 0)
    def _():
        z_buf[...] = jnp.zeros_like(z_buf)
        z_last[...] = jnp.zeros_like(z_last)
        y_s[...] = jnp.zeros_like(y_s)

    @pl.when(jnp.logical_or(first, s == 0))
    def _():
        st_ref[...] = jnp.zeros_like(st_ref)

    z = z_buf[:, :rwkv_in]
    u_ref[...] = z_buf[:, rwkv_in:]
    prev_row = jnp.where(first, 0.0, z_last[7:8, :])
    z_last[...] = z[tb_rows - 8:, :]

    hb = _layer_norm(x_ref[...], ing_ref[...], inb_ref[...]).astype(BF16)

    def project(lo, hi):
        z_buf[:, lo:hi] = _dot(hb, win_ref[:, lo:hi])

    project(0, PROJ_SPLITS[0])
    row = lax.broadcasted_iota(jnp.int32, z.shape, 0)
    z_shift = jnp.where(row == 0, prev_row, pltpu.roll(z, 1, 0))
    zs = z + (z_shift - z) * mu_ref[...]
    r = zs[:, 0:width]
    k = zs[:, width:2 * width]
    v = zs[:, 2 * width:3 * width]
    lora_in = zs[:, 3 * width:3 * width + DECAY_LORA + AAA_LORA]
    gd = zs[:, 3 * width + DECAY_LORA + AAA_LORA:]
    lane = lax.broadcasted_iota(jnp.int32, lora_in.shape, 1)
    lora_in = jnp.where(lane < DECAY_LORA, jnp.tanh(lora_in), lora_in)
    lora = _dot(lora_in.astype(BF16), lora_ref[...])
    gate = _dot(jax.nn.sigmoid(gd).astype(BF16), gl_ref[...])
    seg = seg_ref[...]

    def head_sum(t):
        tb16 = t.astype(BF16)
        return jnp.concatenate([_dot(tb16[:, i:i + LANES], seg) for i in range(0, width, LANES)],
                               axis=1)

    kk = k * kk_ref[...]
    kk_ss = head_sum(kk * kk)
    project(PROJ_SPLITS[0], PROJ_SPLITS[1])

    lw = -DECAY_SCALE * jax.nn.sigmoid(w0_ref[...] + lora[:, :width])
    asig = jax.nn.sigmoid(a0_ref[...] + lora[:, width:])
    kk = kk * lax.rsqrt(jnp.maximum(kk_ss, 1e-24))
    k2 = k * (1.0 + (asig - 1.0) * ka_ref[...])
    bonus = head_sum(r * k2 * rk_ref[...]) * v
    project(PROJ_SPLITS[1], rwkv_in + width)
    r_s[...] = r
    lw_s[...] = lw
    k_s[...] = k2
    v_s[...] = v
    kk_s[...] = kk
    as_s[...] = asig
    g_s[...] = gate
    bonus_s[...] = bonus

    def iota(shape, dim):
        return lax.broadcasted_iota(jnp.int32, shape, dim)

    def head(idx):
        return lax.shift_right_logical(idx, HEAD_SIZE.bit_length() - 1)

    tok = iota((CHUNK, QUAD), 0)
    src_tok = iota((CHUNK, QUAD), 1) & (HEAD_SIZE - 1)
    m0 = (iota((1, LANES), 1) < HEAD_SIZE).astype(F32)
    consts = (tril_ref[...], m0, 1.0 - m0, tok > src_tok, tok >= src_tok,
              iota((PAIR, PAIR), 0) == iota((PAIR, PAIR), 1), tok == src_tok,
              head(iota((QUAD, QUAD), 0)) == head(iota((QUAD, QUAD), 1)),
              head(iota((PAIR, PAIR), 0)) == head(iota((PAIR, PAIR), 1)))
    n_pairs = width // LANES

    def chunk_body(ci_, carry):
        lanes = [slice(p * LANES, (p + 1) * LANES) for p in range(n_pairs)]
        quads = [slice(q * QUAD, (q + 1) * QUAD) for q in range(width // QUAD)]
        h_st = [st_ref[p] for p in range(n_pairs)]

        def group(first_chunk):
            rows = [pl.ds(pl.multiple_of((ci_ * CHUNKS_PER_ITER + first_chunk + u) * CHUNK, CHUNK),
                          CHUNK) for u in range(GROUP_CHUNKS)]
            load = lambda ref: [ref[rw, ln] for rw in rows for ln in quads]
            p_mat, q_mat, g_mat, y0 = yield from _wkv_stages(
                load(r_s), load(lw_s), load(k_s), load(v_s), load(kk_s), load(as_s), consts)
            for u in range(GROUP_CHUNKS):
                yield
                for p in range(n_pairs):
                    i = u * n_pairs + p
                    h_b = h_st[p].astype(BF16)
                    y_s[rows[u], lanes[p]] = _dot(g_mat[i], h_b) + y0[i]
                    h_st[p] = _dot(p_mat[i], h_b) + q_mat[i]

        active = []
        pending = [group(c0) for c0 in range(0, CHUNKS_PER_ITER, GROUP_CHUNKS)]
        tick = 0
        while pending or active:
            if pending and tick % GROUP_LAG == 0:
                active.append(pending.pop(0))
            tick += 1
            for gen in list(active):
                if next(gen, StopIteration) is StopIteration:
                    active.remove(gen)
        for p in range(n_pairs):
            st_ref[p] = h_st[p]
        return carry

    n_iter = jnp.where(s == 0, 0, tb_rows // (CHUNK * CHUNKS_PER_ITER))
    lax.fori_loop(0, n_iter, chunk_body, 0)

    y = y_s[...]
    inv_n = 1.0 / HEAD_SIZE
    mean = head_sum(y) * inv_n
    yc = y - mean
    var = head_sum(yc * yc) * inv_n
    yn = yc * lax.rsqrt(var + GN_EPS) * gng_ref[...] + gnb_ref[...]
    ya_ref[...] = (yn + bonus_s[...]) * g_s[...]


def _pool(cur, prev, pos, pw_ref):
    ext = jnp.concatenate([prev, cur], axis=0)
    outs = []
    for j, win in enumerate(POOL_WINDOWS):
        lanes = slice(j * LANES, (j + 1) * LANES)
        s = ext[:, lanes]
        shift = 1
        while shift < win:
            s = s + pltpu.roll(s, shift, 0)
            shift *= 2
        d = s[POOL_HALO:, :] / jnp.minimum(pos, float(win)) - cur[:, lanes]
        outs.append(_dot(d.astype(BF16), pw_ref[j]))
    return jnp.concatenate(outs, axis=1)


def _post_kernel(x_ref, ya_ref, u_ref, uprev_ref, p_ref, ing_ref, inb_ref, pw_ref, ps_ref, wo_ref,
                 g1_ref, b1_ref, wg_ref, wu_ref, wd_ref, wpg_ref, wpp_ref, g2_ref, b2_ref, o_ref,
                 *, alpha, width, tb_rows, n_split):
    tb = pl.program_id(1)
    sub = tb_rows // n_split
    h1s = []
    for s in range(n_split):
        rows = slice(s * sub, (s + 1) * sub)
        if s == 0:
            prev = jnp.where(tb == 0, 0.0, uprev_ref[...])
        else:
            prev = u_ref[s * sub - POOL_HALO:s * sub, :]
        pos = (tb * tb_rows + s * sub + 1
               + lax.broadcasted_iota(jnp.int32, (sub, 1), 0)).astype(F32)
        y_b = _pool(u_ref[rows, :], prev, pos, pw_ref) * ps_ref[...]
        mix = (_dot(ya_ref[rows, :].astype(BF16), wo_ref[:width, :])
               + _dot(y_b.astype(BF16), wo_ref[width:, :]))
        h = _layer_norm(x_ref[rows, :], ing_ref[...], inb_ref[...])
        h1s.append(_layer_norm(alpha * h + mix, g1_ref[...], b1_ref[...]))
    for s, h1 in enumerate(h1s):
        rows = slice(s * sub, (s + 1) * sub)
        hb = h1.astype(BF16)
        emb = (jax.nn.sigmoid(_dot(hb, wpg_ref[...]))
               * _dot(p_ref[rows, :].astype(BF16), wpp_ref[...]))
        gate = _dot(hb, wg_ref[...])
        act = gate * jax.nn.sigmoid(gate) * _dot(hb, wu_ref[...])
        ffn = _dot(act.astype(BF16), wd_ref[...])
        o_ref[rows, :] = _layer_norm(alpha * h1 + ffn + emb, g2_ref[...], b2_ref[...])


def _resident(shape):
    return pl.BlockSpec(shape, lambda *_: (0,) * len(shape), pipeline_mode=pl.Buffered(1))


def _params(semantics):
    return pltpu.CompilerParams(dimension_semantics=semantics, vmem_limit_bytes=VMEM_LIMIT)


def _layer(x, p_l, ln_in, prm, *, batch, seq):
    n_tok, d_model = x.shape
    width = prm['w0'].shape[-1]
    rwkv_in = 3 * width + DECAY_LORA + AAA_LORA + GATE_LORA
    alpha = (2.0 * prm['depth']) ** 0.25

    tb_rows = FRONT_ROWS
    n_tb = seq // tb_rows
    n_blocks = n_tok // tb_rows
    lora_w = jnp.zeros((DECAY_LORA + AAA_LORA, 2 * width), F32)
    lora_w = lora_w.at[:DECAY_LORA, :width].set(prm['wl_up'])
    lora_w = lora_w.at[DECAY_LORA:, width:].set(prm['al_up']).astype(BF16)
    head_id = jnp.arange(LANES) // HEAD_SIZE
    seg = (head_id[:, None] == head_id[None, :]).astype(BF16)
    tril = jnp.tril(jnp.ones((CHUNK, CHUNK), BF16))
    vec = lambda a: a.reshape(1, -1)
    ln_in = [vec(ln_in[0]), vec(ln_in[1])]
    front_params = ln_in + [prm['w_in'].astype(BF16), vec(prm['mu_shift']), vec(prm['w0']),
                            vec(prm['a0']), lora_w, prm['gl_up'].astype(BF16), vec(prm['k_k']),
                            vec(prm['k_a']), vec(prm['r_k']), vec(prm['lnx_g']),
                            vec(prm['lnx_b']), seg, tril]
    done = lambda c: pl.BlockSpec((tb_rows, c), lambda s: (jnp.maximum(s - 1, 0), 0))
    y_a, u = pl.pallas_call(
        functools.partial(_front_kernel, width=width, tb_rows=tb_rows, n_tb=n_tb),
        grid=(n_blocks + 1,),
        in_specs=[pl.BlockSpec((tb_rows, d_model), lambda s: (jnp.minimum(s, n_blocks - 1), 0))]
        + [_resident(w.shape) for w in front_params],
        out_specs=[done(width), done(width)],
        out_shape=[jax.ShapeDtypeStruct((n_tok, width), F32)] * 2,
        scratch_shapes=[pltpu.VMEM((tb_rows, rwkv_in + width), F32),
                        pltpu.VMEM((8, rwkv_in), F32),
                        pltpu.VMEM((width // LANES, LANES, LANES), F32)]
        + [pltpu.VMEM((tb_rows, width), F32)] * 9,
        compiler_params=_params(("arbitrary",)),
        name="front",
    )(x, *front_params)

    tp = POST_ROWS
    n_tp = seq // tp
    d_ple = p_l.shape[-1]
    blk = lambda c: pl.BlockSpec((tp, c), lambda b, t: (b * n_tp + t, 0))
    halo = pl.BlockSpec(
        (POOL_HALO, width),
        lambda b, t: (jnp.maximum((b * n_tp + t) * (tp // POOL_HALO) - 1, 0), 0))
    weights = ln_in + [prm['pool_w'].astype(BF16), vec(prm['pool_scale']),
                       prm['w_out'].astype(BF16), vec(prm['ln1_g']), vec(prm['ln1_b']),
                       prm['ffn_gate'].astype(BF16), prm['ffn_up'].astype(BF16),
                       prm['ffn_down'].astype(BF16), prm['pl_gate'].astype(BF16),
                       prm['pl_proj'].astype(BF16), vec(prm['ln2_g']), vec(prm['ln2_b'])]
    out = pl.pallas_call(
        functools.partial(_post_kernel, alpha=alpha, width=width, tb_rows=tp, n_split=tp // POST_SUB_ROWS),
        grid=(batch, n_tp),
        in_specs=[blk(d_model), blk(width), blk(width), halo, blk(d_ple)]
        + [_resident(w.shape) for w in weights],
        out_specs=blk(d_model),
        out_shape=jax.ShapeDtypeStruct((n_tok, d_model), F32),
        compiler_params=_params(("parallel", "parallel")),
        name="post",
    )(x, y_a, u, u, p_l, *weights)
    return out


def kernel(x, p, ln_in_g, ln_in_b, w_in, mu_shift, w0, wl_up, a0, al_up, gl_up, k_k, k_a, r_k,
           lnx_g, lnx_b, pool_w, pool_scale, w_out, ln1_g, ln1_b, ffn_gate, ffn_up, ffn_down,
           pl_proj, pl_gate, ln2_g, ln2_b):
    batch, seq, d_model = x.shape
    depth = p.shape[0]
    assert depth == 1, "the fused input norm assumes a single layer"
    stacked = dict(w_in=w_in, mu_shift=mu_shift, w0=w0, wl_up=wl_up, a0=a0, al_up=al_up,
                   gl_up=gl_up, k_k=k_k, k_a=k_a, r_k=r_k, lnx_g=lnx_g, lnx_b=lnx_b,
                   pool_w=pool_w, pool_scale=pool_scale, w_out=w_out, ln1_g=ln1_g, ln1_b=ln1_b,
                   ffn_gate=ffn_gate, ffn_up=ffn_up, ffn_down=ffn_down, pl_proj=pl_proj,
                   pl_gate=pl_gate, ln2_g=ln2_g, ln2_b=ln2_b)
    prm = {name: t[0] for name, t in stacked.items()}
    prm['depth'] = depth
    out = _layer(x.reshape(batch * seq, d_model), p[0].reshape(batch * seq, -1),
                 (ln_in_g, ln_in_b), prm, batch=batch, seq=seq)
    return out.reshape(batch, seq, d_model)
```

```python
import functools

import jax
import jax.numpy as jnp
from jax import lax
from jax.experimental import pallas as pl
from jax.experimental.pallas import tpu as pltpu

F32 = jnp.float32
BF16 = jnp.bfloat16

HEAD_SIZE = 64
POOL_WINDOWS = (2, 4, 8, 16)
DECAY_LORA = 64
AAA_LORA = 64
GATE_LORA = 128
LN_EPS = 1e-5
GN_EPS = 1e-5 * HEAD_SIZE

LANES = 128
CHUNK = 64
PAIR = 2 * CHUNK
QUAD = 4 * HEAD_SIZE
INV_BASE = 8
CHUNKS_PER_ITER = 8
GROUP_CHUNKS = 2
GROUP_LAG = 3
POOL_HALO = 16
VMEM_LIMIT = 56 * 1024 * 1024
FRONT_ROWS = 512
POST_ROWS = 512
POST_SUB_ROWS = 256
PROJ_SPLITS = (1280, 2048)
DECAY_SCALE = 0.6065306597126334


def _dot(a, b):
    return jnp.dot(a, b, preferred_element_type=F32)


def _dot_nt(a, b):
    return lax.dot_general(a, b, (((1,), (1,)), ((), ())), preferred_element_type=F32)


def _layer_norm(x, g, b):
    mu = jnp.mean(x, axis=-1, keepdims=True)
    xc = x - mu
    var = jnp.mean(xc * xc, axis=-1, keepdims=True)
    return xc * lax.rsqrt(var + LN_EPS) * g + b


def _each(fn, *lists):
    return [fn(*xs) for xs in zip(*lists)]


def _wkv_stages(r, lw, k, v, kk, asig, c):
    tril, m0, m1, strict, incl, eye, eye_q, bd_q, bd_p, inv_base, inv_levels = c

    def cumsum(x):
        hi = x.astype(BF16)
        lo = (x - hi.astype(F32)).astype(BF16)
        both = _dot(tril, jnp.concatenate([hi, lo], axis=1))
        return both[:, :QUAD] + both[:, QUAD:]

    def block_diag(x, mask):
        reps = mask.shape[0] // CHUNK
        return jnp.where(mask, jnp.concatenate([x] * reps, axis=0), 0.0).astype(BF16)

    cum = _each(cumsum, lw)
    yield
    g = _each(jnp.exp, cum)
    gprev = _each(lambda cu, x: jnp.exp(cu - x), cum, lw)
    ginv = _each(lambda cu: jnp.exp(-cu), cum)
    g_last = _each(lambda x: x[CHUNK - 1:CHUNK, :], g)
    at = _each(lambda x, y: -(x * y), kk, gprev)
    rt = _each(lambda x, y: x * y, r, g)
    bt = _each(lambda x, y, z: x * y * z, kk, asig, ginv)
    kt = _each(lambda x, y: x * y, k, ginv)
    bl = _each(lambda x, y: x * y, bt, g_last)
    kl = _each(lambda x, y: x * y, kt, g_last)
    yield

    a_all = _each(lambda a_, r_, b_, k_: _dot_nt(
        jnp.concatenate([a_, r_], axis=0).astype(BF16),
        jnp.concatenate([block_diag(b_, bd_q), block_diag(k_, bd_q)], axis=0)), at, rt, bt, kt)
    yield
    a_ab = _each(lambda a_: jnp.where(strict, a_[:CHUNK, :QUAD], 0.0), a_all)
    a_ak = _each(lambda a_: jnp.where(strict, a_[:CHUNK, QUAD:], 0.0), a_all)
    a_rb = _each(lambda a_: jnp.where(incl, a_[CHUNK:, :QUAD], 0.0), a_all)
    a_rk = _each(lambda a_: jnp.where(incl, a_[CHUNK:, QUAD:], 0.0), a_all)

    n_d = _each(lambda n: jnp.where(inv_base, n, 0.0), a_ab)
    t = _each(lambda n: jnp.where(eye_q, 1.0, 0.0) + n, n_d)
    s = _each(lambda n: _dot(n.astype(BF16), block_diag(n, bd_q)), n_d)
    akv = _each(lambda a_, v_: _dot(a_.astype(BF16), block_diag(v_, bd_q)), a_ak, v)
    yield
    ps = _each(lambda t_, s_: _dot(jnp.concatenate([t_, s_], axis=0).astype(BF16),
                                   block_diag(s_, bd_q)), t, s)
    t = _each(lambda t_, ps_: t_ + ps_[:CHUNK], t, ps)
    s = _each(lambda ps_: ps_[CHUNK:], ps)
    yield
    t = _each(lambda t_, s_: t_ + _dot(t_.astype(BF16), block_diag(s_, bd_q)), t, s)
    yield
    for couple in inv_levels:
        x = _each(lambda n, t_: _dot(jnp.where(couple, n, 0.0).astype(BF16),
                                     block_diag(t_, bd_q)), a_ab, t)
        yield
        t = _each(lambda t_, x_: t_ + _dot(t_.astype(BF16), block_diag(x_, bd_q)), t, x)
        yield

    def pairs(tiles):
        return [x[:, i:i + LANES] for x in tiles for i in range(0, QUAD, LANES)]

    def stack(x):
        return jnp.concatenate([x * m0, x * m1], axis=0)

    t_bd = _each(lambda x: block_diag(x, bd_p), pairs(t))
    arb_bd = _each(lambda x: block_diag(x, bd_p), pairs(a_rb))
    ark_bd = _each(lambda x: block_diag(x, bd_p), pairs(a_rk))
    at2, akv2, v2 = _each(stack, pairs(at)), _each(stack, pairs(akv)), _each(stack, pairs(v))
    bl2, kl2 = _each(stack, pairs(bl)), _each(stack, pairs(kl))
    rt_p, gl_p = pairs(rt), pairs(g_last)

    wu = _each(lambda t_, a_, akv_: _dot(
        t_, jnp.concatenate([a_, akv_], axis=1).astype(BF16)), t_bd, at2, akv2)
    yield
    rhs = _each(lambda wu_, v_: jnp.concatenate(
        [wu_, jnp.concatenate([jnp.zeros_like(v_), v_], axis=1)], axis=0).astype(BF16), wu, v2)
    top = _each(lambda b_, k_, rhs_: _dot(
        jnp.concatenate([b_, k_], axis=0).T.astype(BF16), rhs_), bl2, kl2, rhs)
    yield
    bot = _each(lambda b_, k_, rhs_: _dot(jnp.concatenate([b_, k_], axis=1), rhs_),
                arb_bd, ark_bd, rhs)

    p_mat = _each(lambda gl, top_: (jnp.where(eye, gl, 0.0) + top_[:, :LANES]).astype(BF16),
                  gl_p, top)
    q_mat = _each(lambda top_: top_[:, LANES:], top)
    g_mat = _each(lambda rt_, b_: (rt_ + b_[:CHUNK, :LANES] + b_[CHUNK:, :LANES]).astype(BF16),
                  rt_p, bot)
    y0 = _each(lambda b_: b_[:CHUNK, LANES:] + b_[CHUNK:, LANES:], bot)
    return p_mat, q_mat, g_mat, y0


def _front_kernel(x_ref, ing_ref, inb_ref, win_ref, mu_ref, w0_ref, a0_ref, lora_ref, gl_ref,
                  kk_ref, ka_ref, rk_ref, gng_ref, gnb_ref, seg_ref, tril_ref, ya_ref, u_ref,
                  z_buf, z_last, st_ref, r_s, lw_s, k_s, v_s, kk_s, as_s, y_s, g_s, bonus_s,
                  *, width, tb_rows, n_tb):
    s = pl.program_id(0)
    rwkv_in = mu_ref.shape[-1]
    first = lax.rem(s + n_tb - 1, n_tb) == 0

    @pl.when(s == 0)
    def _():
        z_buf[...] = jnp.zeros_like(z_buf)
        z_last[...] = jnp.zeros_like(z_last)
        y_s[...] = jnp.zeros_like(y_s)

    @pl.when(jnp.logical_or(first, s == 0))
    def _():
        st_ref[...] = jnp.zeros_like(st_ref)

    z = z_buf[:, :rwkv_in]
    u_ref[...] = z_buf[:, rwkv_in:]
    prev_row = jnp.where(first, 0.0, z_last[7:8, :])
    z_last[...] = z[tb_rows - 8:, :]

    hb = _layer_norm(x_ref[...], ing_ref[...], inb_ref[...]).astype(BF16)

    def project(lo, hi):
        z_buf[:, lo:hi] = _dot(hb, win_ref[:, lo:hi])

    project(0, PROJ_SPLITS[0])
    row = lax.broadcasted_iota(jnp.int32, z.shape, 0)
    z_shift = jnp.where(row == 0, prev_row, pltpu.roll(z, 1, 0))
    zs = z + (z_shift - z) * mu_ref[...]
    r = zs[:, 0:width]
    k = zs[:, width:2 * width]
    v = zs[:, 2 * width:3 * width]
    lora_in = zs[:, 3 * width:3 * width + DECAY_LORA + AAA_LORA]
    gd = zs[:, 3 * width + DECAY_LORA + AAA_LORA:]
    lane = lax.broadcasted_iota(jnp.int32, lora_in.shape, 1)
    lora_in = jnp.where(lane < DECAY_LORA, jnp.tanh(lora_in), lora_in)
    lora = _dot(lora_in.astype(BF16), lora_ref[...])
    gate = _dot(jax.nn.sigmoid(gd).astype(BF16), gl_ref[...])
    seg = seg_ref[...]

    def head_sum(t):
        tb16 = t.astype(BF16)
        return jnp.concatenate([_dot(tb16[:, i:i + LANES], seg) for i in range(0, width, LANES)],
                               axis=1)

    kk = k * kk_ref[...]
    kk_ss = head_sum(kk * kk)
    project(PROJ_SPLITS[0], PROJ_SPLITS[1])

    lw = -DECAY_SCALE * jax.nn.sigmoid(w0_ref[...] + lora[:, :width])
    asig = jax.nn.sigmoid(a0_ref[...] + lora[:, width:])
    kk = kk * lax.rsqrt(jnp.maximum(kk_ss, 1e-24))
    k2 = k * (1.0 + (asig - 1.0) * ka_ref[...])
    bonus = head_sum(r * k2 * rk_ref[...]) * v
    project(PROJ_SPLITS[1], rwkv_in + width)
    r_s[...] = r
    lw_s[...] = lw
    k_s[...] = k2
    v_s[...] = v
    kk_s[...] = kk
    as_s[...] = asig
    g_s[...] = gate
    bonus_s[...] = bonus

    def iota(shape, dim):
        return lax.broadcasted_iota(jnp.int32, shape, dim)

    def head(idx):
        return lax.shift_right_logical(idx, HEAD_SIZE.bit_length() - 1)

    tok = iota((CHUNK, QUAD), 0)
    src_tok = iota((CHUNK, QUAD), 1) & (HEAD_SIZE - 1)
    m0 = (iota((1, LANES), 1) < HEAD_SIZE).astype(F32)

    def same_block(size):
        shift = size.bit_length() - 1
        return (lax.shift_right_logical(tok, shift) == lax.shift_right_logical(src_tok, shift))

    consts = (tril_ref[...], m0, 1.0 - m0, tok > src_tok, tok >= src_tok,
              iota((PAIR, PAIR), 0) == iota((PAIR, PAIR), 1), tok == src_tok,
              head(iota((QUAD, QUAD), 0)) == head(iota((QUAD, QUAD), 1)),
              head(iota((PAIR, PAIR), 0)) == head(iota((PAIR, PAIR), 1)),
              same_block(INV_BASE),
              [jnp.logical_and(same_block(2 * b), jnp.logical_not(same_block(b)))
               for b in (INV_BASE << i for i in range(CHUNK.bit_length() - INV_BASE.bit_length()))])
    n_pairs = width // LANES

    def chunk_body(ci_, carry):
        lanes = [slice(p * LANES, (p + 1) * LANES) for p in range(n_pairs)]
        quads = [slice(q * QUAD, (q + 1) * QUAD) for q in range(width // QUAD)]
        h_st = [st_ref[p] for p in range(n_pairs)]

        def group(first_chunk):
            rows = [pl.ds(pl.multiple_of((ci_ * CHUNKS_PER_ITER + first_chunk + u) * CHUNK, CHUNK),
                          CHUNK) for u in range(GROUP_CHUNKS)]
            load = lambda ref: [ref[rw, ln] for rw in rows for ln in quads]
            p_mat, q_mat, g_mat, y0 = yield from _wkv_stages(
                load(r_s), load(lw_s), load(k_s), load(v_s), load(kk_s), load(as_s), consts)
            for u in range(GROUP_CHUNKS):
                yield
                for p in range(n_pairs):
                    i = u * n_pairs + p
                    h_b = h_st[p].astype(BF16)
                    y_s[rows[u], lanes[p]] = _dot(g_mat[i], h_b) + y0[i]
                    h_st[p] = _dot(p_mat[i], h_b) + q_mat[i]

        active = []
        pending = [group(c0) for c0 in range(0, CHUNKS_PER_ITER, GROUP_CHUNKS)]
        tick = 0
        while pending or active:
            if pending and tick % GROUP_LAG == 0:
                active.append(pending.pop(0))
            tick += 1
            for gen in list(active):
                if next(gen, StopIteration) is StopIteration:
                    active.remove(gen)
        for p in range(n_pairs):
            st_ref[p] = h_st[p]
        return carry

    n_iter = jnp.where(s == 0, 0, tb_rows // (CHUNK * CHUNKS_PER_ITER))
    lax.fori_loop(0, n_iter, chunk_body, 0)

    y = y_s[...]
    inv_n = 1.0 / HEAD_SIZE
    mean = head_sum(y) * inv_n
    yc = y - mean
    var = head_sum(yc * yc) * inv_n
    yn = yc * lax.rsqrt(var + GN_EPS) * gng_ref[...] + gnb_ref[...]
    ya_ref[...] = (yn + bonus_s[...]) * g_s[...]


def _pool(cur, prev, pos, pw_ref):
    ext = jnp.concatenate([prev, cur], axis=0)
    outs = []
    for j, win in enumerate(POOL_WINDOWS):
        lanes = slice(j * LANES, (j + 1) * LANES)
        s = ext[:, lanes]
        shift = 1
        while shift < win:
            s = s + pltpu.roll(s, shift, 0)
            shift *= 2
        d = s[POOL_HALO:, :] / jnp.minimum(pos, float(win)) - cur[:, lanes]
        outs.append(_dot(d.astype(BF16), pw_ref[j]))
    return jnp.concatenate(outs, axis=1)


def _post_kernel(x_ref, ya_ref, u_ref, uprev_ref, p_ref, ing_ref, inb_ref, pw_ref, ps_ref, wo_ref,
                 g1_ref, b1_ref, wg_ref, wu_ref, wd_ref, wpg_ref, wpp_ref, g2_ref, b2_ref, o_ref,
                 *, alpha, width, tb_rows, n_split):
    tb = pl.program_id(1)
    sub = tb_rows // n_split
    h1s = []
    for s in range(n_split):
        rows = slice(s * sub, (s + 1) * sub)
        if s == 0:
            prev = jnp.where(tb == 0, 0.0, uprev_ref[...])
        else:
            prev = u_ref[s * sub - POOL_HALO:s * sub, :]
        pos = (tb * tb_rows + s * sub + 1
               + lax.broadcasted_iota(jnp.int32, (sub, 1), 0)).astype(F32)
        y_b = _pool(u_ref[rows, :], prev, pos, pw_ref) * ps_ref[...]
        mix = (_dot(ya_ref[rows, :].astype(BF16), wo_ref[:width, :])
               + _dot(y_b.astype(BF16), wo_ref[width:, :]))
        h = _layer_norm(x_ref[rows, :], ing_ref[...], inb_ref[...])
        h1s.append(_layer_norm(alpha * h + mix, g1_ref[...], b1_ref[...]))
    for s, h1 in enumerate(h1s):
        rows = slice(s * sub, (s + 1) * sub)
        hb = h1.astype(BF16)
        emb = (jax.nn.sigmoid(_dot(hb, wpg_ref[...]))
               * _dot(p_ref[rows, :].astype(BF16), wpp_ref[...]))
        gate = _dot(hb, wg_ref[...])
        act = gate * jax.nn.sigmoid(gate) * _dot(hb, wu_ref[...])
        ffn = _dot(act.astype(BF16), wd_ref[...])
        o_ref[rows, :] = _layer_norm(alpha * h1 + ffn + emb, g2_ref[...], b2_ref[...])


def _resident(shape):
    return pl.BlockSpec(shape, lambda *_: (0,) * len(shape), pipeline_mode=pl.Buffered(1))


def _params(semantics):
    return pltpu.CompilerParams(dimension_semantics=semantics, vmem_limit_bytes=VMEM_LIMIT)


def _layer(x, p_l, ln_in, prm, *, batch, seq):
    n_tok, d_model = x.shape
    width = prm['w0'].shape[-1]
    rwkv_in = 3 * width + DECAY_LORA + AAA_LORA + GATE_LORA
    alpha = (2.0 * prm['depth']) ** 0.25

    tb_rows = FRONT_ROWS
    n_tb = seq // tb_rows
    n_blocks = n_tok // tb_rows
    lora_w = jnp.zeros((DECAY_LORA + AAA_LORA, 2 * width), F32)
    lora_w = lora_w.at[:DECAY_LORA, :width].set(prm['wl_up'])
    lora_w = lora_w.at[DECAY_LORA:, width:].set(prm['al_up']).astype(BF16)
    head_id = jnp.arange(LANES) // HEAD_SIZE
    seg = (head_id[:, None] == head_id[None, :]).astype(BF16)
    tril = jnp.tril(jnp.ones((CHUNK, CHUNK), BF16))
    vec = lambda a: a.reshape(1, -1)
    ln_in = [vec(ln_in[0]), vec(ln_in[1])]
    front_params = ln_in + [prm['w_in'].astype(BF16), vec(prm['mu_shift']), vec(prm['w0']),
                            vec(prm['a0']), lora_w, prm['gl_up'].astype(BF16), vec(prm['k_k']),
                            vec(prm['k_a']), vec(prm['r_k']), vec(prm['lnx_g']),
                            vec(prm['lnx_b']), seg, tril]
    done = lambda c: pl.BlockSpec((tb_rows, c), lambda s: (jnp.maximum(s - 1, 0), 0))
    y_a, u = pl.pallas_call(
        functools.partial(_front_kernel, width=width, tb_rows=tb_rows, n_tb=n_tb),
        grid=(n_blocks + 1,),
        in_specs=[pl.BlockSpec((tb_rows, d_model), lambda s: (jnp.minimum(s, n_blocks - 1), 0))]
        + [_resident(w.shape) for w in front_params],
        out_specs=[done(width), done(width)],
        out_shape=[jax.ShapeDtypeStruct((n_tok, width), F32)] * 2,
        scratch_shapes=[pltpu.VMEM((tb_rows, rwkv_in + width), F32),
                        pltpu.VMEM((8, rwkv_in), F32),
                        pltpu.VMEM((width // LANES, LANES, LANES), F32)]
        + [pltpu.VMEM((tb_rows, width), F32)] * 9,
        compiler_params=_params(("arbitrary",)),
        name="front",
    )(x, *front_params)

    tp = POST_ROWS
    n_tp = seq // tp
    d_ple = p_l.shape[-1]
    blk = lambda c: pl.BlockSpec((tp, c), lambda b, t: (b * n_tp + t, 0))
    halo = pl.BlockSpec(
        (POOL_HALO, width),
        lambda b, t: (jnp.maximum((b * n_tp + t) * (tp // POOL_HALO) - 1, 0), 0))
    weights = ln_in + [prm['pool_w'].astype(BF16), vec(prm['pool_scale']),
                       prm['w_out'].astype(BF16), vec(prm['ln1_g']), vec(prm['ln1_b']),
                       prm['ffn_gate'].astype(BF16), prm['ffn_up'].astype(BF16),
                       prm['ffn_down'].astype(BF16), prm['pl_gate'].astype(BF16),
                       prm['pl_proj'].astype(BF16), vec(prm['ln2_g']), vec(prm['ln2_b'])]
    out = pl.pallas_call(
        functools.partial(_post_kernel, alpha=alpha, width=width, tb_rows=tp, n_split=tp // POST_SUB_ROWS),
        grid=(batch, n_tp),
        in_specs=[blk(d_model), blk(width), blk(width), halo, blk(d_ple)]
        + [_resident(w.shape) for w in weights],
        out_specs=blk(d_model),
        out_shape=jax.ShapeDtypeStruct((n_tok, d_model), F32),
        compiler_params=_params(("parallel", "parallel")),
        name="post",
    )(x, y_a, u, u, p_l, *weights)
    return out


def kernel(x, p, ln_in_g, ln_in_b, w_in, mu_shift, w0, wl_up, a0, al_up, gl_up, k_k, k_a, r_k,
           lnx_g, lnx_b, pool_w, pool_scale, w_out, ln1_g, ln1_b, ffn_gate, ffn_up, ffn_down,
           pl_proj, pl_gate, ln2_g, ln2_b):
    batch, seq, d_model = x.shape
    depth = p.shape[0]
    assert depth == 1, "the fused input norm assumes a single layer"
    stacked = dict(w_in=w_in, mu_shift=mu_shift, w0=w0, wl_up=wl_up, a0=a0, al_up=al_up,
                   gl_up=gl_up, k_k=k_k, k_a=k_a, r_k=r_k, lnx_g=lnx_g, lnx_b=lnx_b,
                   pool_w=pool_w, pool_scale=pool_scale, w_out=w_out, ln1_g=ln1_g, ln1_b=ln1_b,
                   ffn_gate=ffn_gate, ffn_up=ffn_up, ffn_down=ffn_down, pl_proj=pl_proj,
                   pl_gate=pl_gate, ln2_g=ln2_g, ln2_b=ln2_b)
    prm = {name: t[0] for name, t in stacked.items()}
    prm['depth'] = depth
    out = _layer(x.reshape(batch * seq, d_model), p[0].reshape(batch * seq, -1),
                 (ln_in_g, ln_in_b), prm, batch=batch, seq=seq)
    return out.reshape(batch, seq, d_model)
```

```python
import functools

import jax
import jax.numpy as jnp
from jax import lax
from jax.experimental import pallas as pl
from jax.experimental.pallas import tpu as pltpu

F32 = jnp.float32
BF16 = jnp.bfloat16

HEAD_SIZE = 64
POOL_WINDOWS = (2, 4, 8, 16)
DECAY_LORA = 64
AAA_LORA = 64
GATE_LORA = 128
LN_EPS = 1e-5
GN_EPS = 1e-5 * HEAD_SIZE

LANES = 128
CHUNK = 64
PAIR = 2 * CHUNK
QUAD = 4 * HEAD_SIZE
INV_BASE = 8
CHUNKS_PER_ITER = 8
GROUP_CHUNKS = 2
GROUP_LAG = 3
POOL_HALO = 16
VMEM_LIMIT = 56 * 1024 * 1024
FRONT_ROWS = 512
POST_ROWS = 512
POST_SUB_ROWS = 256
PROJ_SPLITS = (1280, 2048)
DECAY_SCALE = 0.6065306597126334


def _dot(a, b):
    return jnp.dot(a, b, preferred_element_type=F32)


def _dot_nt(a, b):
    return lax.dot_general(a, b, (((1,), (1,)), ((), ())), preferred_element_type=F32)


def _layer_norm(x, g, b):
    mu = jnp.mean(x, axis=-1, keepdims=True)
    xc = x - mu
    var = jnp.mean(xc * xc, axis=-1, keepdims=True)
    return xc * lax.rsqrt(var + LN_EPS) * g + b


def _each(fn, *lists):
    return [fn(*xs) for xs in zip(*lists)]


def _wkv_stages(r, lw, k, v, kk, asig, c):
    tril, m0, m1, strict, incl, eye, eye_q, bd_q, bd_p, inv_base, inv_levels = c

    def cumsum(x):
        hi = x.astype(BF16)
        lo = (x - hi.astype(F32)).astype(BF16)
        both = _dot(tril, jnp.concatenate([hi, lo], axis=1))
        return both[:, :QUAD] + both[:, QUAD:]

    def block_diag(x, mask):
        reps = mask.shape[0] // CHUNK
        return jnp.where(mask, jnp.concatenate([x] * reps, axis=0), 0.0).astype(BF16)

    cum = _each(cumsum, lw)
    yield
    g = _each(jnp.exp, cum)
    gprev = _each(lambda cu, x: jnp.exp(cu - x), cum, lw)
    ginv = _each(lambda cu: jnp.exp(-cu), cum)
    g_last = _each(lambda x: x[CHUNK - 1:CHUNK, :], g)
    at = _each(lambda x, y: -(x * y), kk, gprev)
    rt = _each(lambda x, y: x * y, r, g)
    bt = _each(lambda x, y, z: x * y * z, kk, asig, ginv)
    kt = _each(lambda x, y: x * y, k, ginv)
    bl = _each(lambda x, y: x * y, bt, g_last)
    kl = _each(lambda x, y: x * y, kt, g_last)
    yield

    a_all = _each(lambda a_, r_, b_, k_: _dot_nt(
        jnp.concatenate([a_, r_], axis=0).astype(BF16),
        jnp.concatenate([block_diag(b_, bd_q), block_diag(k_, bd_q)], axis=0)), at, rt, bt, kt)
    yield
    a_ab = _each(lambda a_: jnp.where(strict, a_[:CHUNK, :QUAD], 0.0), a_all)
    a_ak = _each(lambda a_: jnp.where(strict, a_[:CHUNK, QUAD:], 0.0), a_all)
    a_rb = _each(lambda a_: jnp.where(incl, a_[CHUNK:, :QUAD], 0.0), a_all)
    a_rk = _each(lambda a_: jnp.where(incl, a_[CHUNK:, QUAD:], 0.0), a_all)

    n_d = _each(lambda n: jnp.where(inv_base, n, 0.0), a_ab)
    t = _each(lambda n: jnp.where(eye_q, 1.0, 0.0) + n, n_d)
    s = _each(lambda n: _dot(n.astype(BF16), block_diag(n, bd_q)), n_d)
    akv = _each(lambda a_, v_: _dot(a_.astype(BF16), block_diag(v_, bd_q)), a_ak, v)
    yield
    ps = _each(lambda t_, s_: _dot(jnp.concatenate([t_, s_], axis=0).astype(BF16),
                                   block_diag(s_, bd_q)), t, s)
    t = _each(lambda t_, ps_: t_ + ps_[:CHUNK], t, ps)
    s = _each(lambda ps_: ps_[CHUNK:], ps)
    yield
    t = _each(lambda t_, s_: t_ + _dot(t_.astype(BF16), block_diag(s_, bd_q)), t, s)
    yield
    coupled = [_each(lambda n: jnp.where(couple, n, 0.0), a_ab) for couple in inv_levels]
    prod = _each(lambda t_, *cs: _dot(jnp.concatenate(cs, axis=0).astype(BF16),
                                      block_diag(t_, bd_q)), t, *coupled)
    pending = [_each(lambda p: p[i * CHUNK:(i + 1) * CHUNK], prod) for i in range(len(coupled))]
    yield
    while pending:
        x = pending.pop(0)
        prod = _each(lambda x_, *ls: _dot(jnp.concatenate(ls, axis=0).astype(BF16),
                                          block_diag(x_, bd_q)), x, t, *pending)
        t = _each(lambda t_, p: t_ + p[:CHUNK], t, prod)
        pending = [_each(lambda c_, p: c_ + p[(i + 1) * CHUNK:(i + 2) * CHUNK], c, prod)
                   for i, c in enumerate(pending)]
        yield

    def pairs(tiles):
        return [x[:, i:i + LANES] for x in tiles for i in range(0, QUAD, LANES)]

    def stack(x):
        return jnp.concatenate([x * m0, x * m1], axis=0)

    t_bd = _each(lambda x: block_diag(x, bd_p), pairs(t))
    arb_bd = _each(lambda x: block_diag(x, bd_p), pairs(a_rb))
    ark_bd = _each(lambda x: block_diag(x, bd_p), pairs(a_rk))
    at2, akv2, v2 = _each(stack, pairs(at)), _each(stack, pairs(akv)), _each(stack, pairs(v))
    bl2, kl2 = _each(stack, pairs(bl)), _each(stack, pairs(kl))
    rt_p, gl_p = pairs(rt), pairs(g_last)

    wu = _each(lambda t_, a_, akv_: _dot(
        t_, jnp.concatenate([a_, akv_], axis=1).astype(BF16)), t_bd, at2, akv2)
    yield
    rhs = _each(lambda wu_, v_: jnp.concatenate(
        [wu_, jnp.concatenate([jnp.zeros_like(v_), v_], axis=1)], axis=0).astype(BF16), wu, v2)
    top = _each(lambda b_, k_, rhs_: _dot(
        jnp.concatenate([b_, k_], axis=0).T.astype(BF16), rhs_), bl2, kl2, rhs)
    yield
    bot = _each(lambda b_, k_, rhs_: _dot(jnp.concatenate([b_, k_], axis=1), rhs_),
                arb_bd, ark_bd, rhs)

    p_mat = _each(lambda gl, top_: (jnp.where(eye, gl, 0.0) + top_[:, :LANES]).astype(BF16),
                  gl_p, top)
    q_mat = _each(lambda top_: top_[:, LANES:], top)
    g_mat = _each(lambda rt_, b_: (rt_ + b_[:CHUNK, :LANES] + b_[CHUNK:, :LANES]).astype(BF16),
                  rt_p, bot)
    y0 = _each(lambda b_: b_[:CHUNK, LANES:] + b_[CHUNK:, LANES:], bot)
    return p_mat, q_mat, g_mat, y0


def _front_kernel(x_ref, ing_ref, inb_ref, win_ref, mu_ref, w0_ref, a0_ref, lora_ref, gl_ref,
                  kk_ref, ka_ref, rk_ref, gng_ref, gnb_ref, seg_ref, tril_ref, ya_ref, u_ref,
                  z_buf, z_last, st_ref, r_s, lw_s, k_s, v_s, kk_s, as_s, y_s, g_s, bonus_s,
                  *, width, tb_rows, n_tb):
    s = pl.program_id(0)
    rwkv_in = mu_ref.shape[-1]
    first = lax.rem(s + n_tb - 1, n_tb) == 0

    @pl.when(s == 0)
    def _():
        z_buf[...] = jnp.zeros_like(z_buf)
        z_last[...] = jnp.zeros_like(z_last)
        y_s[...] = jnp.zeros_like(y_s)

    @pl.when(jnp.logical_or(first, s == 0))
    def _():
        st_ref[...] = jnp.zeros_like(st_ref)

    z = z_buf[:, :rwkv_in]
    u_ref[...] = z_buf[:, rwkv_in:]
    prev_row = jnp.where(first, 0.0, z_last[7:8, :])
    z_last[...] = z[tb_rows - 8:, :]

    hb = _layer_norm(x_ref[...], ing_ref[...], inb_ref[...]).astype(BF16)

    def project(lo, hi):
        z_buf[:, lo:hi] = _dot(hb, win_ref[:, lo:hi])

    project(0, PROJ_SPLITS[0])
    row = lax.broadcasted_iota(jnp.int32, z.shape, 0)
    z_shift = jnp.where(row == 0, prev_row, pltpu.roll(z, 1, 0))
    zs = z + (z_shift - z) * mu_ref[...]
    r = zs[:, 0:width]
    k = zs[:, width:2 * width]
    v = zs[:, 2 * width:3 * width]
    lora_in = zs[:, 3 * width:3 * width + DECAY_LORA + AAA_LORA]
    gd = zs[:, 3 * width + DECAY_LORA + AAA_LORA:]
    lane = lax.broadcasted_iota(jnp.int32, lora_in.shape, 1)
    lora_in = jnp.where(lane < DECAY_LORA, jnp.tanh(lora_in), lora_in)
    lora = _dot(lora_in.astype(BF16), lora_ref[...])
    gate = _dot(jax.nn.sigmoid(gd).astype(BF16), gl_ref[...])
    seg = seg_ref[...]

    def head_sum(t):
        tb16 = t.astype(BF16)
        return jnp.concatenate([_dot(tb16[:, i:i + LANES], seg) for i in range(0, width, LANES)],
                               axis=1)

    kk = k * kk_ref[...]
    kk_ss = head_sum(kk * kk)
    project(PROJ_SPLITS[0], PROJ_SPLITS[1])

    lw = -DECAY_SCALE * jax.nn.sigmoid(w0_ref[...] + lora[:, :width])
    asig = jax.nn.sigmoid(a0_ref[...] + lora[:, width:])
    kk = kk * lax.rsqrt(jnp.maximum(kk_ss, 1e-24))
    k2 = k * (1.0 + (asig - 1.0) * ka_ref[...])
    bonus = head_sum(r * k2 * rk_ref[...]) * v
    project(PROJ_SPLITS[1], rwkv_in + width)
    r_s[...] = r
    lw_s[...] = lw
    k_s[...] = k2
    v_s[...] = v
    kk_s[...] = kk
    as_s[...] = asig
    g_s[...] = gate
    bonus_s[...] = bonus

    def iota(shape, dim):
        return lax.broadcasted_iota(jnp.int32, shape, dim)

    def head(idx):
        return lax.shift_right_logical(idx, HEAD_SIZE.bit_length() - 1)

    tok = iota((CHUNK, QUAD), 0)
    src_tok = iota((CHUNK, QUAD), 1) & (HEAD_SIZE - 1)
    m0 = (iota((1, LANES), 1) < HEAD_SIZE).astype(F32)

    def same_block(size):
        shift = size.bit_length() - 1
        return (lax.shift_right_logical(tok, shift) == lax.shift_right_logical(src_tok, shift))

    consts = (tril_ref[...], m0, 1.0 - m0, tok > src_tok, tok >= src_tok,
              iota((PAIR, PAIR), 0) == iota((PAIR, PAIR), 1), tok == src_tok,
              head(iota((QUAD, QUAD), 0)) == head(iota((QUAD, QUAD), 1)),
              head(iota((PAIR, PAIR), 0)) == head(iota((PAIR, PAIR), 1)),
              same_block(INV_BASE),
              [jnp.logical_and(same_block(2 * b), jnp.logical_not(same_block(b)))
               for b in (INV_BASE << i for i in range(CHUNK.bit_length() - INV_BASE.bit_length()))])
    n_pairs = width // LANES

    def chunk_body(ci_, carry):
        lanes = [slice(p * LANES, (p + 1) * LANES) for p in range(n_pairs)]
        quads = [slice(q * QUAD, (q + 1) * QUAD) for q in range(width // QUAD)]
        h_st = [st_ref[p] for p in range(n_pairs)]

        def group(first_chunk):
            rows = [pl.ds(pl.multiple_of((ci_ * CHUNKS_PER_ITER + first_chunk + u) * CHUNK, CHUNK),
                          CHUNK) for u in range(GROUP_CHUNKS)]
            load = lambda ref: [ref[rw, ln] for rw in rows for ln in quads]
            p_mat, q_mat, g_mat, y0 = yield from _wkv_stages(
                load(r_s), load(lw_s), load(k_s), load(v_s), load(kk_s), load(as_s), consts)
            for u in range(GROUP_CHUNKS):
                yield
                for p in range(n_pairs):
                    i = u * n_pairs + p
                    h_b = h_st[p].astype(BF16)
                    y_s[rows[u], lanes[p]] = _dot(g_mat[i], h_b) + y0[i]
                    h_st[p] = _dot(p_mat[i], h_b) + q_mat[i]

        active = []
        pending = [group(c0) for c0 in range(0, CHUNKS_PER_ITER, GROUP_CHUNKS)]
        tick = 0
        while pending or active:
            if pending and tick % GROUP_LAG == 0:
                active.append(pending.pop(0))
            tick += 1
            for gen in list(active):
                if next(gen, StopIteration) is StopIteration:
                    active.remove(gen)
        for p in range(n_pairs):
            st_ref[p] = h_st[p]
        return carry

    n_iter = jnp.where(s == 0, 0, tb_rows // (CHUNK * CHUNKS_PER_ITER))
    lax.fori_loop(0, n_iter, chunk_body, 0)

    y = y_s[...]
    inv_n = 1.0 / HEAD_SIZE
    mean = head_sum(y) * inv_n
    yc = y - mean
    var = head_sum(yc * yc) * inv_n
    yn = yc * lax.rsqrt(var + GN_EPS) * gng_ref[...] + gnb_ref[...]
    ya_ref[...] = (yn + bonus_s[...]) * g_s[...]


def _pool(cur, prev, pos, pw_ref):
    ext = jnp.concatenate([prev, cur], axis=0)
    outs = []
    for j, win in enumerate(POOL_WINDOWS):
        lanes = slice(j * LANES, (j + 1) * LANES)
        s = ext[:, lanes]
        shift = 1
        while shift < win:
            s = s + pltpu.roll(s, shift, 0)
            shift *= 2
        d = s[POOL_HALO:, :] / jnp.minimum(pos, float(win)) - cur[:, lanes]
        outs.append(_dot(d.astype(BF16), pw_ref[j]))
    return jnp.concatenate(outs, axis=1)


def _post_kernel(x_ref, ya_ref, u_ref, uprev_ref, p_ref, ing_ref, inb_ref, pw_ref, ps_ref, wo_ref,
                 g1_ref, b1_ref, wg_ref, wu_ref, wd_ref, wpg_ref, wpp_ref, g2_ref, b2_ref, o_ref,
                 *, alpha, width, tb_rows, n_split):
    tb = pl.program_id(1)
    sub = tb_rows // n_split
    h1s = []
    for s in range(n_split):
        rows = slice(s * sub, (s + 1) * sub)
        if s == 0:
            prev = jnp.where(tb == 0, 0.0, uprev_ref[...])
        else:
            prev = u_ref[s * sub - POOL_HALO:s * sub, :]
        pos = (tb * tb_rows + s * sub + 1
               + lax.broadcasted_iota(jnp.int32, (sub, 1), 0)).astype(F32)
        y_b = _pool(u_ref[rows, :], prev, pos, pw_ref) * ps_ref[...]
        mix = (_dot(ya_ref[rows, :].astype(BF16), wo_ref[:width, :])
               + _dot(y_b.astype(BF16), wo_ref[width:, :]))
        h = _layer_norm(x_ref[rows, :], ing_ref[...], inb_ref[...])
        h1s.append(_layer_norm(alpha * h + mix, g1_ref[...], b1_ref[...]))
    for s, h1 in enumerate(h1s):
        rows = slice(s * sub, (s + 1) * sub)
        hb = h1.astype(BF16)
        emb = (jax.nn.sigmoid(_dot(hb, wpg_ref[...]))
               * _dot(p_ref[rows, :].astype(BF16), wpp_ref[...]))
        gate = _dot(hb, wg_ref[...])
        act = gate * jax.nn.sigmoid(gate) * _dot(hb, wu_ref[...])
        ffn = _dot(act.astype(BF16), wd_ref[...])
        o_ref[rows, :] = _layer_norm(alpha * h1 + ffn + emb, g2_ref[...], b2_ref[...])


def _resident(shape):
    return pl.BlockSpec(shape, lambda *_: (0,) * len(shape), pipeline_mode=pl.Buffered(1))


def _params(semantics):
    return pltpu.CompilerParams(dimension_semantics=semantics, vmem_limit_bytes=VMEM_LIMIT)


def _layer(x, p_l, ln_in, prm, *, batch, seq):
    n_tok, d_model = x.shape
    width = prm['w0'].shape[-1]
    rwkv_in = 3 * width + DECAY_LORA + AAA_LORA + GATE_LORA
    alpha = (2.0 * prm['depth']) ** 0.25

    tb_rows = FRONT_ROWS
    n_tb = seq // tb_rows
    n_blocks = n_tok // tb_rows
    lora_w = jnp.zeros((DECAY_LORA + AAA_LORA, 2 * width), F32)
    lora_w = lora_w.at[:DECAY_LORA, :width].set(prm['wl_up'])
    lora_w = lora_w.at[DECAY_LORA:, width:].set(prm['al_up']).astype(BF16)
    head_id = jnp.arange(LANES) // HEAD_SIZE
    seg = (head_id[:, None] == head_id[None, :]).astype(BF16)
    tril = jnp.tril(jnp.ones((CHUNK, CHUNK), BF16))
    vec = lambda a: a.reshape(1, -1)
    ln_in = [vec(ln_in[0]), vec(ln_in[1])]
    front_params = ln_in + [prm['w_in'].astype(BF16), vec(prm['mu_shift']), vec(prm['w0']),
                            vec(prm['a0']), lora_w, prm['gl_up'].astype(BF16), vec(prm['k_k']),
                            vec(prm['k_a']), vec(prm['r_k']), vec(prm['lnx_g']),
                            vec(prm['lnx_b']), seg, tril]
    done = lambda c: pl.BlockSpec((tb_rows, c), lambda s: (jnp.maximum(s - 1, 0), 0))
    y_a, u = pl.pallas_call(
        functools.partial(_front_kernel, width=width, tb_rows=tb_rows, n_tb=n_tb),
        grid=(n_blocks + 1,),
        in_specs=[pl.BlockSpec((tb_rows, d_model), lambda s: (jnp.minimum(s, n_blocks - 1), 0))]
        + [_resident(w.shape) for w in front_params],
        out_specs=[done(width), done(width)],
        out_shape=[jax.ShapeDtypeStruct((n_tok, width), F32)] * 2,
        scratch_shapes=[pltpu.VMEM((tb_rows, rwkv_in + width), F32),
                        pltpu.VMEM((8, rwkv_in), F32),
                        pltpu.VMEM((width // LANES, LANES, LANES), F32)]
        + [pltpu.VMEM((tb_rows, width), F32)] * 9,
        compiler_params=_params(("arbitrary",)),
        name="front",
    )(x, *front_params)

    tp = POST_ROWS
    n_tp = seq // tp
    d_ple = p_l.shape[-1]
    blk = lambda c: pl.BlockSpec((tp, c), lambda b, t: (b * n_tp + t, 0))
    halo = pl.BlockSpec(
        (POOL_HALO, width),
        lambda b, t: (jnp.maximum((b * n_tp + t) * (tp // POOL_HALO) - 1, 0), 0))
    weights = ln_in + [prm['pool_w'].astype(BF16), vec(prm['pool_scale']),
                       prm['w_out'].astype(BF16), vec(prm['ln1_g']), vec(prm['ln1_b']),
                       prm['ffn_gate'].astype(BF16), prm['ffn_up'].astype(BF16),
                       prm['ffn_down'].astype(BF16), prm['pl_gate'].astype(BF16),
                       prm['pl_proj'].astype(BF16), vec(prm['ln2_g']), vec(prm['ln2_b'])]
    out = pl.pallas_call(
        functools.partial(_post_kernel, alpha=alpha, width=width, tb_rows=tp, n_split=tp // POST_SUB_ROWS),
        grid=(batch, n_tp),
        in_specs=[blk(d_model), blk(width), blk(width), halo, blk(d_ple)]
        + [_resident(w.shape) for w in weights],
        out_specs=blk(d_model),
        out_shape=jax.ShapeDtypeStruct((n_tok, d_model), F32),
        compiler_params=_params(("parallel", "parallel")),
        name="post",
    )(x, y_a, u, u, p_l, *weights)
    return out


def kernel(x, p, ln_in_g, ln_in_b, w_in, mu_shift, w0, wl_up, a0, al_up, gl_up, k_k, k_a, r_k,
           lnx_g, lnx_b, pool_w, pool_scale, w_out, ln1_g, ln1_b, ffn_gate, ffn_up, ffn_down,
           pl_proj, pl_gate, ln2_g, ln2_b):
    batch, seq, d_model = x.shape
    depth = p.shape[0]
    assert depth == 1, "the fused input norm assumes a single layer"
    stacked = dict(w_in=w_in, mu_shift=mu_shift, w0=w0, wl_up=wl_up, a0=a0, al_up=al_up,
                   gl_up=gl_up, k_k=k_k, k_a=k_a, r_k=r_k, lnx_g=lnx_g, lnx_b=lnx_b,
                   pool_w=pool_w, pool_scale=pool_scale, w_out=w_out, ln1_g=ln1_g, ln1_b=ln1_b,
                   ffn_gate=ffn_gate, ffn_up=ffn_up, ffn_down=ffn_down, pl_proj=pl_proj,
                   pl_gate=pl_gate, ln2_g=ln2_g, ln2_b=ln2_b)
    prm = {name: t[0] for name, t in stacked.items()}
    prm['depth'] = depth
    out = _layer(x.reshape(batch * seq, d_model), p[0].reshape(batch * seq, -1),
                 (ln_in_g, ln_in_b), prm, batch=batch, seq=seq)
    return out.reshape(batch, seq, d_model)
```

```python
import functools

import jax
import jax.numpy as jnp
from jax import lax
from jax.experimental import pallas as pl
from jax.experimental.pallas import tpu as pltpu

F32 = jnp.float32
BF16 = jnp.bfloat16

HEAD_SIZE = 64
POOL_WINDOWS = (2, 4, 8, 16)
DECAY_LORA = 64
AAA_LORA = 64
GATE_LORA = 128
LN_EPS = 1e-5
GN_EPS = 1e-5 * HEAD_SIZE

LANES = 128
CHUNK = 64
PAIR = 2 * CHUNK
QUAD = 4 * HEAD_SIZE
INV_BASE = 8
CHUNKS_PER_ITER = 8
GROUP_CHUNKS = 2
GROUP_LAG = 3
POOL_HALO = 16
VMEM_LIMIT = 56 * 1024 * 1024
FRONT_ROWS = 512
POST_ROWS = 512
POST_SUB_ROWS = 256
PROJ_SPLITS = (1280, 2048)
DECAY_SCALE = 0.6065306597126334


def _dot(a, b):
    return jnp.dot(a, b, preferred_element_type=F32)


def _dot_nt(a, b):
    return lax.dot_general(a, b, (((1,), (1,)), ((), ())), preferred_element_type=F32)


def _layer_norm(x, g, b):
    mu = jnp.mean(x, axis=-1, keepdims=True)
    xc = x - mu
    var = jnp.mean(xc * xc, axis=-1, keepdims=True)
    return xc * lax.rsqrt(var + LN_EPS) * g + b


def _each(fn, *lists):
    return [fn(*xs) for xs in zip(*lists)]


def _wkv_stages(r, lw, k, v, kk, asig, c):
    tril, m0, m1, strict, incl, eye, eye_q, bd_q, bd_p, inv_base, inv_levels = c

    def cumsum(x):
        hi = x.astype(BF16)
        lo = (x - hi.astype(F32)).astype(BF16)
        both = _dot(tril, jnp.concatenate([hi, lo], axis=1))
        return both[:, :QUAD] + both[:, QUAD:]

    def block_diag(x, mask):
        reps = mask.shape[0] // CHUNK
        return jnp.where(mask, jnp.concatenate([x] * reps, axis=0), 0.0).astype(BF16)

    cum = _each(cumsum, lw)
    yield
    g = _each(jnp.exp, cum)
    gprev = _each(lambda cu, x: jnp.exp(cu - x), cum, lw)
    ginv = _each(lambda cu: jnp.exp(-cu), cum)
    g_last = _each(lambda x: x[CHUNK - 1:CHUNK, :], g)
    at = _each(lambda x, y: -(x * y), kk, gprev)
    rt = _each(lambda x, y: x * y, r, g)
    bt = _each(lambda x, y, z: x * y * z, kk, asig, ginv)
    kt = _each(lambda x, y: x * y, k, ginv)
    bl = _each(lambda x, y: x * y, bt, g_last)
    kl = _each(lambda x, y: x * y, kt, g_last)
    yield

    a_all = _each(lambda a_, r_, b_, k_: _dot_nt(
        jnp.concatenate([a_, r_], axis=0).astype(BF16),
        jnp.concatenate([block_diag(b_, bd_q), block_diag(k_, bd_q)], axis=0)), at, rt, bt, kt)
    yield
    a_ab = _each(lambda a_: jnp.where(strict, a_[:CHUNK, :QUAD], 0.0), a_all)
    a_ak = _each(lambda a_: jnp.where(strict, a_[:CHUNK, QUAD:], 0.0), a_all)
    a_rb = _each(lambda a_: jnp.where(incl, a_[CHUNK:, :QUAD], 0.0), a_all)
    a_rk = _each(lambda a_: jnp.where(incl, a_[CHUNK:, QUAD:], 0.0), a_all)

    n_d = _each(lambda n: jnp.where(inv_base, n, 0.0), a_ab)
    t = _each(lambda n: jnp.where(eye_q, 1.0, 0.0) + n, n_d)
    s = _each(lambda n: _dot(n.astype(BF16), block_diag(n, bd_q)), n_d)
    akv = _each(lambda a_, v_: _dot(a_.astype(BF16), block_diag(v_, bd_q)), a_ak, v)
    yield
    ps = _each(lambda t_, s_: _dot(jnp.concatenate([t_, s_], axis=0).astype(BF16),
                                   block_diag(s_, bd_q)), t, s)
    t = _each(lambda t_, ps_: t_ + ps_[:CHUNK], t, ps)
    s = _each(lambda ps_: ps_[CHUNK:], ps)
    yield
    t = _each(lambda t_, s_: t_ + _dot(t_.astype(BF16), block_diag(s_, bd_q)), t, s)
    yield
    coupled = [_each(lambda n: jnp.where(couple, n, 0.0), a_ab) for couple in inv_levels]
    prod = _each(lambda t_, *cs: _dot(jnp.concatenate(cs, axis=0).astype(BF16),
                                      block_diag(t_, bd_q)), t, *coupled)
    pending = [_each(lambda p: p[i * CHUNK:(i + 1) * CHUNK], prod) for i in range(len(coupled))]
    yield
    while pending:
        x = pending.pop(0)
        prod = _each(lambda x_, *ls: _dot(jnp.concatenate(ls, axis=0).astype(BF16),
                                          block_diag(x_, bd_q)), x, t, *pending)
        t = _each(lambda t_, p: t_ + p[:CHUNK], t, prod)
        pending = [_each(lambda c_, p: c_ + p[(i + 1) * CHUNK:(i + 2) * CHUNK], c, prod)
                   for i, c in enumerate(pending)]
        yield

    def pairs(tiles):
        return [x[:, i:i + LANES] for x in tiles for i in range(0, QUAD, LANES)]

    def stack(x):
        return jnp.concatenate([x * m0, x * m1], axis=0)

    t_bd = _each(lambda x: block_diag(x, bd_p), pairs(t))
    arb_bd = _each(lambda x: block_diag(x, bd_p), pairs(a_rb))
    ark_bd = _each(lambda x: block_diag(x, bd_p), pairs(a_rk))
    at2, akv2, v2 = _each(stack, pairs(at)), _each(stack, pairs(akv)), _each(stack, pairs(v))
    bl2, kl2 = _each(stack, pairs(bl)), _each(stack, pairs(kl))
    rt_p, gl_p = pairs(rt), pairs(g_last)

    wu = _each(lambda t_, a_, akv_: _dot(
        t_, jnp.concatenate([a_, akv_], axis=1).astype(BF16)), t_bd, at2, akv2)
    yield
    rhs = _each(lambda wu_, v_: jnp.concatenate(
        [wu_, jnp.concatenate([jnp.zeros_like(v_), v_], axis=1)], axis=0).astype(BF16), wu, v2)
    top = _each(lambda b_, k_, rhs_: _dot(
        jnp.concatenate([b_, k_], axis=0).T.astype(BF16), rhs_), bl2, kl2, rhs)
    yield
    bot = _each(lambda b_, k_, rhs_: _dot(jnp.concatenate([b_, k_], axis=1), rhs_),
                arb_bd, ark_bd, rhs)

    p_mat = _each(lambda gl, top_: (jnp.where(eye, gl, 0.0) + top_[:, :LANES]).astype(BF16),
                  gl_p, top)
    q_mat = _each(lambda top_: top_[:, LANES:], top)
    g_mat = _each(lambda rt_, b_: (rt_ + b_[:CHUNK, :LANES] + b_[CHUNK:, :LANES]).astype(BF16),
                  rt_p, bot)
    y0 = _each(lambda b_: b_[:CHUNK, LANES:] + b_[CHUNK:, LANES:], bot)
    return p_mat, q_mat, g_mat, y0


def _front_kernel(x_ref, ing_ref, inb_ref, win_ref, mu_ref, w0_ref, a0_ref, lora_ref, gl_ref,
                  kk_ref, ka_ref, rk_ref, gng_ref, gnb_ref, seg_ref, tril_ref, ya_ref, u_ref,
                  z_buf, z_last, st_ref, r_s, lw_s, k_s, v_s, kk_s, as_s, y_s, g_s, bonus_s,
                  *, width, tb_rows, n_tb):
    s = pl.program_id(0)
    rwkv_in = mu_ref.shape[-1]
    first = lax.rem(s + n_tb - 1, n_tb) == 0

    @pl.when(s == 0)
    def _():
        z_buf[...] = jnp.zeros_like(z_buf)
        z_last[...] = jnp.zeros_like(z_last)
        y_s[...] = jnp.zeros_like(y_s)

    @pl.when(jnp.logical_or(first, s == 0))
    def _():
        st_ref[...] = jnp.zeros_like(st_ref)

    z = z_buf[:, :rwkv_in]
    u_ref[...] = z_buf[:, rwkv_in:]
    prev_row = jnp.where(first, 0.0, z_last[7:8, :])
    z_last[...] = z[tb_rows - 8:, :]

    hb = _layer_norm(x_ref[...], ing_ref[...], inb_ref[...]).astype(BF16)

    def project(lo, hi):
        z_buf[:, lo:hi] = _dot(hb, win_ref[:, lo:hi])

    project(0, PROJ_SPLITS[0])
    row = lax.broadcasted_iota(jnp.int32, z.shape, 0)
    z_shift = jnp.where(row == 0, prev_row, pltpu.roll(z, 1, 0))
    zs = z + (z_shift - z) * mu_ref[...]
    r = zs[:, 0:width]
    k = zs[:, width:2 * width]
    v = zs[:, 2 * width:3 * width]
    lora_in = zs[:, 3 * width:3 * width + DECAY_LORA + AAA_LORA]
    gd = zs[:, 3 * width + DECAY_LORA + AAA_LORA:]
    lane = lax.broadcasted_iota(jnp.int32, lora_in.shape, 1)
    lora_in = jnp.where(lane < DECAY_LORA, jnp.tanh(lora_in), lora_in)
    lora = _dot(lora_in.astype(BF16), lora_ref[...])
    gate = _dot(jax.nn.sigmoid(gd).astype(BF16), gl_ref[...])
    seg = seg_ref[...]

    def head_sum(t):
        tb16 = t.astype(BF16)
        return jnp.concatenate([_dot(tb16[:, i:i + LANES], seg) for i in range(0, width, LANES)],
                               axis=1)

    kk = k * kk_ref[...]
    kk_ss = head_sum(kk * kk)
    project(PROJ_SPLITS[0], PROJ_SPLITS[1])

    lw = -DECAY_SCALE * jax.nn.sigmoid(w0_ref[...] + lora[:, :width])
    asig = jax.nn.sigmoid(a0_ref[...] + lora[:, width:])
    kk = kk * lax.rsqrt(jnp.maximum(kk_ss, 1e-24))
    k2 = k * (1.0 + (asig - 1.0) * ka_ref[...])
    bonus = head_sum(r * k2 * rk_ref[...]) * v
    project(PROJ_SPLITS[1], rwkv_in + width)
    r_s[...] = r
    lw_s[...] = lw
    k_s[...] = k2
    v_s[...] = v
    kk_s[...] = kk
    as_s[...] = asig
    g_s[...] = gate
    bonus_s[...] = bonus

    def iota(shape, dim):
        return lax.broadcasted_iota(jnp.int32, shape, dim)

    def head(idx):
        return lax.shift_right_logical(idx, HEAD_SIZE.bit_length() - 1)

    tok = iota((CHUNK, QUAD), 0)
    src_tok = iota((CHUNK, QUAD), 1) & (HEAD_SIZE - 1)
    m0 = (iota((1, LANES), 1) < HEAD_SIZE).astype(F32)

    def same_block(size):
        shift = size.bit_length() - 1
        return (lax.shift_right_logical(tok, shift) == lax.shift_right_logical(src_tok, shift))

    consts = (tril_ref[...], m0, 1.0 - m0, tok > src_tok, tok >= src_tok,
              iota((PAIR, PAIR), 0) == iota((PAIR, PAIR), 1), tok == src_tok,
              head(iota((QUAD, QUAD), 0)) == head(iota((QUAD, QUAD), 1)),
              head(iota((PAIR, PAIR), 0)) == head(iota((PAIR, PAIR), 1)),
              same_block(INV_BASE),
              [jnp.logical_and(same_block(2 * b), jnp.logical_not(same_block(b)))
               for b in (INV_BASE << i for i in range(CHUNK.bit_length() - INV_BASE.bit_length()))])
    n_pairs = width // LANES

    def chunk_body(ci_, carry):
        lanes = [slice(p * LANES, (p + 1) * LANES) for p in range(n_pairs)]
        quads = [slice(q * QUAD, (q + 1) * QUAD) for q in range(width // QUAD)]
        h_st = [st_ref[p] for p in range(n_pairs)]

        def group(first_chunk):
            rows = [pl.ds(pl.multiple_of((ci_ * CHUNKS_PER_ITER + first_chunk + u) * CHUNK, CHUNK),
                          CHUNK) for u in range(GROUP_CHUNKS)]
            load = lambda ref: [ref[rw, ln] for rw in rows for ln in quads]
            p_mat, q_mat, g_mat, y0 = yield from _wkv_stages(
                load(r_s), load(lw_s), load(k_s), load(v_s), load(kk_s), load(as_s), consts)
            for u in range(GROUP_CHUNKS):
                yield
                for p in range(n_pairs):
                    i = u * n_pairs + p
                    h_b = h_st[p].astype(BF16)
                    y_s[rows[u], lanes[p]] = _dot(g_mat[i], h_b) + y0[i]
                    h_st[p] = _dot(p_mat[i], h_b) + q_mat[i]

        active = []
        pending = [group(c0) for c0 in range(0, CHUNKS_PER_ITER, GROUP_CHUNKS)]
        tick = 0
        while pending or active:
            if pending and tick % GROUP_LAG == 0:
                active.append(pending.pop(0))
            tick += 1
            for gen in list(active):
                if next(gen, StopIteration) is StopIteration:
                    active.remove(gen)
        for p in range(n_pairs):
            st_ref[p] = h_st[p]
        return carry

    n_iter = jnp.where(s == 0, 0, tb_rows // (CHUNK * CHUNKS_PER_ITER))
    lax.fori_loop(0, n_iter, chunk_body, 0)

    y = y_s[...]
    inv_n = 1.0 / HEAD_SIZE
    mean = head_sum(y) * inv_n
    yc = y - mean
    var = head_sum(yc * yc) * inv_n
    yn = yc * lax.rsqrt(var + GN_EPS) * gng_ref[...] + gnb_ref[...]
    ya_ref[...] = ((yn + bonus_s[...]) * g_s[...]).astype(BF16)


def _pool(cur, prev, pos, pw_ref):
    ext = jnp.concatenate([prev, cur], axis=0)
    outs = []
    for j, win in enumerate(POOL_WINDOWS):
        lanes = slice(j * LANES, (j + 1) * LANES)
        s = ext[:, lanes]
        shift = 1
        while shift < win:
            s = s + pltpu.roll(s, shift, 0)
            shift *= 2
        d = s[POOL_HALO:, :] / jnp.minimum(pos, float(win)) - cur[:, lanes]
        outs.append(_dot(d.astype(BF16), pw_ref[j]))
    return jnp.concatenate(outs, axis=1)


def _post_kernel(x_ref, ya_ref, u_ref, uprev_ref, p_ref, ing_ref, inb_ref, pw_ref, ps_ref, wo_ref,
                 g1_ref, b1_ref, wg_ref, wu_ref, wd_ref, wpg_ref, wpp_ref, g2_ref, b2_ref, o_ref,
                 *, alpha, width, tb_rows, n_split):
    tb = pl.program_id(1)
    sub = tb_rows // n_split
    h1s = []
    for s in range(n_split):
        rows = slice(s * sub, (s + 1) * sub)
        if s == 0:
            prev = jnp.where(tb == 0, 0.0, uprev_ref[...])
        else:
            prev = u_ref[s * sub - POOL_HALO:s * sub, :]
        pos = (tb * tb_rows + s * sub + 1
               + lax.broadcasted_iota(jnp.int32, (sub, 1), 0)).astype(F32)
        y_b = _pool(u_ref[rows, :], prev, pos, pw_ref) * ps_ref[...]
        mix = (_dot(ya_ref[rows, :], wo_ref[:width, :])
               + _dot(y_b.astype(BF16), wo_ref[width:, :]))
        h = _layer_norm(x_ref[rows, :], ing_ref[...], inb_ref[...])
        h1s.append(_layer_norm(alpha * h + mix, g1_ref[...], b1_ref[...]))
    for s, h1 in enumerate(h1s):
        rows = slice(s * sub, (s + 1) * sub)
        hb = h1.astype(BF16)
        emb = (jax.nn.sigmoid(_dot(hb, wpg_ref[...]))
               * _dot(p_ref[rows, :].astype(BF16), wpp_ref[...]))
        gate = _dot(hb, wg_ref[...])
        act = gate * jax.nn.sigmoid(gate) * _dot(hb, wu_ref[...])
        ffn = _dot(act.astype(BF16), wd_ref[...])
        o_ref[rows, :] = _layer_norm(alpha * h1 + ffn + emb, g2_ref[...], b2_ref[...])


def _resident(shape):
    return pl.BlockSpec(shape, lambda *_: (0,) * len(shape), pipeline_mode=pl.Buffered(1))


def _params(semantics):
    return pltpu.CompilerParams(dimension_semantics=semantics, vmem_limit_bytes=VMEM_LIMIT)


def _layer(x, p_l, ln_in, prm, *, batch, seq):
    n_tok, d_model = x.shape
    width = prm['w0'].shape[-1]
    rwkv_in = 3 * width + DECAY_LORA + AAA_LORA + GATE_LORA
    alpha = (2.0 * prm['depth']) ** 0.25

    tb_rows = FRONT_ROWS
    n_tb = seq // tb_rows
    n_blocks = n_tok // tb_rows
    lora_w = jnp.zeros((DECAY_LORA + AAA_LORA, 2 * width), F32)
    lora_w = lora_w.at[:DECAY_LORA, :width].set(prm['wl_up'])
    lora_w = lora_w.at[DECAY_LORA:, width:].set(prm['al_up']).astype(BF16)
    head_id = jnp.arange(LANES) // HEAD_SIZE
    seg = (head_id[:, None] == head_id[None, :]).astype(BF16)
    tril = jnp.tril(jnp.ones((CHUNK, CHUNK), BF16))
    vec = lambda a: a.reshape(1, -1)
    ln_in = [vec(ln_in[0]), vec(ln_in[1])]
    front_params = ln_in + [prm['w_in'].astype(BF16), vec(prm['mu_shift']), vec(prm['w0']),
                            vec(prm['a0']), lora_w, prm['gl_up'].astype(BF16), vec(prm['k_k']),
                            vec(prm['k_a']), vec(prm['r_k']), vec(prm['lnx_g']),
                            vec(prm['lnx_b']), seg, tril]
    done = lambda c: pl.BlockSpec((tb_rows, c), lambda s: (jnp.maximum(s - 1, 0), 0))
    y_a, u = pl.pallas_call(
        functools.partial(_front_kernel, width=width, tb_rows=tb_rows, n_tb=n_tb),
        grid=(n_blocks + 1,),
        in_specs=[pl.BlockSpec((tb_rows, d_model), lambda s: (jnp.minimum(s, n_blocks - 1), 0))]
        + [_resident(w.shape) for w in front_params],
        out_specs=[done(width), done(width)],
        out_shape=[jax.ShapeDtypeStruct((n_tok, width), BF16),
                   jax.ShapeDtypeStruct((n_tok, width), F32)],
        scratch_shapes=[pltpu.VMEM((tb_rows, rwkv_in + width), F32),
                        pltpu.VMEM((8, rwkv_in), F32),
                        pltpu.VMEM((width // LANES, LANES, LANES), F32)]
        + [pltpu.VMEM((tb_rows, width), F32)] * 9,
        compiler_params=_params(("arbitrary",)),
        name="front",
    )(x, *front_params)

    tp = POST_ROWS
    n_tp = seq // tp
    d_ple = p_l.shape[-1]
    blk = lambda c: pl.BlockSpec((tp, c), lambda b, t: (b * n_tp + t, 0))
    halo = pl.BlockSpec(
        (POOL_HALO, width),
        lambda b, t: (jnp.maximum((b * n_tp + t) * (tp // POOL_HALO) - 1, 0), 0))
    weights = ln_in + [prm['pool_w'].astype(BF16), vec(prm['pool_scale']),
                       prm['w_out'].astype(BF16), vec(prm['ln1_g']), vec(prm['ln1_b']),
                       prm['ffn_gate'].astype(BF16), prm['ffn_up'].astype(BF16),
                       prm['ffn_down'].astype(BF16), prm['pl_gate'].astype(BF16),
                       prm['pl_proj'].astype(BF16), vec(prm['ln2_g']), vec(prm['ln2_b'])]
    out = pl.pallas_call(
        functools.partial(_post_kernel, alpha=alpha, width=width, tb_rows=tp, n_split=tp // POST_SUB_ROWS),
        grid=(batch, n_tp),
        in_specs=[blk(d_model), blk(width), blk(width), halo, blk(d_ple)]
        + [_resident(w.shape) for w in weights],
        out_specs=blk(d_model),
        out_shape=jax.ShapeDtypeStruct((n_tok, d_model), F32),
        compiler_params=_params(("parallel", "parallel")),
        name="post",
    )(x, y_a, u, u, p_l, *weights)
    return out


def kernel(x, p, ln_in_g, ln_in_b, w_in, mu_shift, w0, wl_up, a0, al_up, gl_up, k_k, k_a, r_k,
           lnx_g, lnx_b, pool_w, pool_scale, w_out, ln1_g, ln1_b, ffn_gate, ffn_up, ffn_down,
           pl_proj, pl_gate, ln2_g, ln2_b):
    batch, seq, d_model = x.shape
    depth = p.shape[0]
    assert depth == 1, "the fused input norm assumes a single layer"
    stacked = dict(w_in=w_in, mu_shift=mu_shift, w0=w0, wl_up=wl_up, a0=a0, al_up=al_up,
                   gl_up=gl_up, k_k=k_k, k_a=k_a, r_k=r_k, lnx_g=lnx_g, lnx_b=lnx_b,
                   pool_w=pool_w, pool_scale=pool_scale, w_out=w_out, ln1_g=ln1_g, ln1_b=ln1_b,
                   ffn_gate=ffn_gate, ffn_up=ffn_up, ffn_down=ffn_down, pl_proj=pl_proj,
                   pl_gate=pl_gate, ln2_g=ln2_g, ln2_b=ln2_b)
    prm = {name: t[0] for name, t in stacked.items()}
    prm['depth'] = depth
    out = _layer(x.reshape(batch * seq, d_model), p[0].reshape(batch * seq, -1),
                 (ln_in_g, ln_in_b), prm, batch=batch, seq=seq)
    return out.reshape(batch, seq, d_model)
```
